```python
import math
import jax, jax.numpy as jnp
from jax import lax
import numpy as np

D_MODEL = 1024
BATCH = 4
SEQ = 8192
DEPTH = 1
DEC_BATCH = 16
DEC_SEQ = 32
PAST_LEN = 1024

CHUNK = 64
Q_BLOCK = 128
RET_HEADS = 4
RET_DK = 128
RET_DV = 256
DA_HEADS = 4
DA_DK = 128
DA_DV = 256
D_FF = 2816
CONV_W = 3
PLE_DIM = 256
ROPE_THETA = 10000.0
EPS = 1e-6

RET_QK = RET_HEADS * RET_DK
RET_V = RET_HEADS * RET_DV
DA_QK = DA_HEADS * 2 * DA_DK
DA_V = DA_HEADS * DA_DV
IN_WIDTHS = (RET_QK, RET_QK, RET_V, RET_V, DA_QK, DA_QK, DA_V, D_MODEL, D_MODEL)
D_IN = sum(IN_WIDTHS)
SPLIT_IDX = [int(v) for v in np.cumsum(IN_WIDTHS)[:-1]]

kernel_name = 'hybrid_retention_diffattn_stream_step'


def rms_norm(x, g=None):
    xf = x.astype(jnp.float32)
    y = xf * lax.rsqrt(jnp.mean(xf * xf, axis=-1, keepdims=True) + EPS)
    if g is not None:
        y = y * g.astype(jnp.float32)
    return y.astype(x.dtype)


def rope(x, pos):
    d = x.shape[-1]
    inv = ROPE_THETA ** (-jnp.arange(0, d, 2, dtype=jnp.float32) / d)
    ang = pos.astype(jnp.float32)[:, None] * inv[None, :]
    ang = jnp.concatenate([ang, ang], axis=-1)
    shape = (1, pos.shape[0]) + (1,) * (x.ndim - 3) + (d,)
    cos = jnp.cos(ang).reshape(shape)
    sin = jnp.sin(ang).reshape(shape)
    xf = x.astype(jnp.float32)
    rot = jnp.concatenate([-xf[..., d // 2:], xf[..., :d // 2]], axis=-1)
    return (xf * cos + rot * sin).astype(x.dtype)


def ret_log_gamma():
    return jnp.log(1.0 - jnp.exp2(-5.0 - jnp.arange(RET_HEADS, dtype=jnp.float32)))


def retention_chunk(q, k, v, S, log_g):
    C = q.shape[1]
    idx = jnp.arange(C, dtype=jnp.float32)
    diff = idx[:, None] - idx[None, :]
    decay = jnp.where(diff >= 0, jnp.exp(log_g[:, None, None] * jnp.maximum(diff, 0.0)), 0.0)
    scores = jnp.einsum('bihd,bjhd->bhij', q, k) * decay[None]
    intra = jnp.einsum('bhij,bjhe->bihe', scores, v)
    q_decay = jnp.exp((idx + 1.0)[:, None] * log_g[None, :])
    inter = jnp.einsum('bihd,bhde->bihe', q, S) * q_decay[None, :, :, None]
    k_decay = jnp.exp((C - 1.0 - idx)[:, None] * log_g[None, :])
    S_new = jnp.exp(C * log_g)[None, :, None, None] * S + jnp.einsum('bjhd,bjhe->bhde', k * k_decay[None, :, :, None], v)
    return intra + inter, S_new


def retention_prompt(q, k, v, log_g):
    B, S = q.shape[:2]
    n = S // CHUNK
    def to_chunks(t):
        return t.reshape((B, n, CHUNK) + t.shape[2:]).transpose(1, 0, 2, 3, 4)
    def step(state, xs):
        qc, kc, vc = xs
        o, state = retention_chunk(qc, kc, vc, state, log_g)
        return state, o
    S0 = jnp.zeros((B, RET_HEADS, RET_DK, RET_DV), jnp.float32)
    S_fin, o = lax.scan(step, S0, (to_chunks(q), to_chunks(k), to_chunks(v)))
    o = o.transpose(1, 0, 2, 3, 4).reshape(B, S, RET_HEADS, RET_DV)
    return o, S_fin


def diff_core(q, k, v, mask, lam):
    s = jnp.einsum('bqhmd,bkhmd->bhmqk', q, k, preferred_element_type=jnp.float32) * (DA_DK ** -0.5)
    if mask is not None:
        s = jnp.where(mask, s, jnp.finfo(jnp.float32).min)
    p = jax.nn.softmax(s, axis=-1)
    a = p[:, :, 0] - lam * p[:, :, 1]
    return jnp.einsum('bhqk,bkhe->bqhe', a.astype(v.dtype), v)


def diff_attn_prompt(q, k, v, lam):
    B, S = q.shape[:2]
    nb = S // Q_BLOCK
    qb = q.reshape(B, nb, Q_BLOCK, DA_HEADS, 2, DA_DK).transpose(1, 0, 2, 3, 4, 5)
    k_chunk = jnp.arange(S) // CHUNK
    def one_block(args):
        qi, bi = args
        q_chunk = (bi * Q_BLOCK + jnp.arange(Q_BLOCK)) // CHUNK
        mask = k_chunk[None, :] <= q_chunk[:, None]
        return diff_core(qi, k, v, mask, lam)
    o = lax.map(one_block, (qb, jnp.arange(nb)))
    return o.transpose(1, 0, 2, 3, 4).reshape(B, S, DA_HEADS, DA_DV)


def layer(x, ple, pos, past, lp, layer_idx):
    B, T, _ = x.shape
    h = rms_norm(x, lp['g_mix'])
    z = h @ lp['w_in']
    rq, rk, rv, rg, dq, dk, dv, ga, gb = jnp.split(z, SPLIT_IDX, axis=-1)

    log_g = ret_log_gamma()
    rq = rope(rq.reshape(B, T, RET_HEADS, RET_DK), pos).astype(jnp.float32)
    rk = (rope(rk.reshape(B, T, RET_HEADS, RET_DK), pos) * (RET_DK ** -0.5)).astype(jnp.float32)
    rv = rv.reshape(B, T, RET_HEADS, RET_DV).astype(jnp.float32)
    if past is None:
        o_r, S_new = retention_prompt(rq, rk, rv, log_g)
    else:
        o_r, S_new = retention_chunk(rq, rk, rv, past[2].astype(jnp.float32), log_g)
    o_r = rms_norm(o_r).astype(x.dtype).reshape(B, T, RET_V) * jax.nn.silu(rg)
    branch_a = o_r @ lp['w_ret_br']

    lam_init = 0.8 - 0.6 * math.exp(-0.3 * layer_idx)
    lam = (jnp.exp(jnp.sum(lp['lam_q1'].astype(jnp.float32) * lp['lam_k1'].astype(jnp.float32)))
           - jnp.exp(jnp.sum(lp['lam_q2'].astype(jnp.float32) * lp['lam_k2'].astype(jnp.float32))) + lam_init)
    dq = rope(rms_norm(dq.reshape(B, T, DA_HEADS, 2, DA_DK), lp['g_qn']), pos)
    dk = rope(rms_norm(dk.reshape(B, T, DA_HEADS, 2, DA_DK), lp['g_kn']), pos)
    dv = dv.reshape(B, T, DA_HEADS, DA_DV)
    if past is None:
        o_d = diff_attn_prompt(dq, dk, dv, lam)
    else:
        k_all = jnp.concatenate([past[0].astype(dk.dtype), dk], axis=1)
        v_all = jnp.concatenate([past[1].astype(dv.dtype), dv], axis=1)
        o_d = diff_core(dq, k_all, v_all, None, lam)
    o_d = (rms_norm(o_d, lp['g_dnorm']) * (1.0 - lam_init)).reshape(B, T, DA_V)
    branch_b = o_d @ lp['w_da_br']

    m = jax.nn.sigmoid(ga) * branch_a + jax.nn.sigmoid(gb) * branch_b
    x = x + m @ lp['w_out']

    u = rms_norm(x, lp['g_ffn']) @ lp['w_up']
    buf = jnp.zeros((B, CONV_W - 1, 2 * D_FF), u.dtype) if past is None else past[3].astype(u.dtype)
    u_full = jnp.concatenate([buf, u], axis=1)
    cw = lp['conv_w']
    c = sum(cw[j] * u_full[:, j:j + T] for j in range(CONV_W)) + lp['conv_b']
    cg, cv = jnp.split(c, 2, axis=-1)
    x = x + (jax.nn.silu(cg) * cv) @ lp['w_down']
    conv_new = u_full[:, -(CONV_W - 1):]

    x = x + jax.nn.sigmoid(rms_norm(x, lp['g_ple']) @ lp['w_ple_gate']) * (ple @ lp['w_ple'])
    return x, (dk, dv, S_new, conv_new)


def setup_inputs(seed: int = 0) -> dict:
    key = jax.random.key(seed)
    ks = iter(jax.random.split(key, 32))
    def nrm(shape, scale):
        return jax.random.normal(next(ks), shape, jnp.float32) * scale
    def gain(shape):
        return 1.0 + nrm(shape, 0.02)
    return {
        'x_prompt': nrm((BATCH, SEQ, D_MODEL), 1.0),
        'x_sample': nrm((DEC_BATCH, DEC_SEQ, D_MODEL), 1.0),
        'cache_k': nrm((DEPTH, DEC_BATCH, PAST_LEN, DA_HEADS, 2, DA_DK), 1.0),
        'cache_v': nrm((DEPTH, DEC_BATCH, PAST_LEN, DA_HEADS, DA_DV), 1.0),
        'state_ret': nrm((DEPTH, DEC_BATCH, RET_HEADS, RET_DK, RET_DV), 0.3),
        'state_conv': nrm((DEPTH, DEC_BATCH, CONV_W - 1, 2 * D_FF), 1.0),
        'p_prompt': nrm((DEPTH, BATCH, SEQ, PLE_DIM), 1.0),
        'p_sample': nrm((DEPTH, DEC_BATCH, DEC_SEQ, PLE_DIM), 1.0),
        'g_mix': gain((DEPTH, D_MODEL)),
        'w_in': nrm((DEPTH, D_MODEL, D_IN), D_MODEL ** -0.5),
        'g_qn': gain((DEPTH, DA_DK)),
        'g_kn': gain((DEPTH, DA_DK)),
        'lam_q1': nrm((DEPTH, DA_DK), 0.1),
        'lam_k1': nrm((DEPTH, DA_DK), 0.1),
        'lam_q2': nrm((DEPTH, DA_DK), 0.1),
        'lam_k2': nrm((DEPTH, DA_DK), 0.1),
        'g_dnorm': gain((DEPTH, DA_DV)),
        'w_ret_br': nrm((DEPTH, RET_V, D_MODEL), RET_V ** -0.5),
        'w_da_br': nrm((DEPTH, DA_V, D_MODEL), DA_V ** -0.5),
        'w_out': nrm((DEPTH, D_MODEL, D_MODEL), D_MODEL ** -0.5),
        'g_ffn': gain((DEPTH, D_MODEL)),
        'w_up': nrm((DEPTH, D_MODEL, 2 * D_FF), D_MODEL ** -0.5),
        'conv_w': nrm((DEPTH, CONV_W, 2 * D_FF), CONV_W ** -0.5),
        'conv_b': nrm((DEPTH, 2 * D_FF), 0.02),
        'w_down': nrm((DEPTH, D_FF, D_MODEL), D_FF ** -0.5),
        'g_ple': gain((DEPTH, D_MODEL)),
        'w_ple_gate': nrm((DEPTH, D_MODEL, D_MODEL), D_MODEL ** -0.5),
        'w_ple': nrm((DEPTH, PLE_DIM, D_MODEL), PLE_DIM ** -0.5),
    }


def reference(x_prompt, x_sample, cache_k, cache_v, state_ret, state_conv, p_prompt, p_sample,
              g_mix, w_in, g_qn, g_kn, lam_q1, lam_k1, lam_q2, lam_k2, g_dnorm, w_ret_br, w_da_br,
              w_out, g_ffn, w_up, conv_w, conv_b, w_down, g_ple, w_ple_gate, w_ple):
    pos_p = jnp.arange(x_prompt.shape[1])
    pos_s = PAST_LEN + jnp.arange(x_sample.shape[1])
    yp, ys = x_prompt, x_sample
    kp, ks_, vp, vs, rp, rs, cp, cs = [], [], [], [], [], [], [], []
    for i in range(DEPTH):
        lp = dict(g_mix=g_mix[i], w_in=w_in[i], g_qn=g_qn[i], g_kn=g_kn[i], lam_q1=lam_q1[i],
                  lam_k1=lam_k1[i], lam_q2=lam_q2[i], lam_k2=lam_k2[i], g_dnorm=g_dnorm[i],
                  w_ret_br=w_ret_br[i], w_da_br=w_da_br[i], w_out=w_out[i], g_ffn=g_ffn[i],
                  w_up=w_up[i], conv_w=conv_w[i], conv_b=conv_b[i], w_down=w_down[i],
                  g_ple=g_ple[i], w_ple_gate=w_ple_gate[i], w_ple=w_ple[i])
        yp, (k1, v1, s1, c1) = layer(yp, p_prompt[i], pos_p, None, lp, i)
        ys, (k2, v2, s2, c2) = layer(ys, p_sample[i], pos_s,
                                     (cache_k[i], cache_v[i], state_ret[i], state_conv[i]), lp, i)
        kp.append(k1); ks_.append(k2); vp.append(v1); vs.append(v2)
        rp.append(s1); rs.append(s2); cp.append(c1); cs.append(c2)
    return (yp, ys, jnp.stack(kp), jnp.stack(ks_), jnp.stack(vp), jnp.stack(vs),
            jnp.stack(rp), jnp.stack(rs), jnp.stack(cp), jnp.stack(cs))
```

```python
import functools
import math

import jax
import jax.numpy as jnp
import numpy as np
from jax import lax
from jax.experimental import pallas as pl
from jax.experimental.pallas import tpu as pltpu

D_MODEL = 1024
CHUNK = 64
RET_HEADS = 4
RET_DK = 128
RET_DV = 256
DA_HEADS = 4
DA_DK = 128
DA_DV = 256
D_FF = 2816
CONV_W = 3
ROPE_THETA = 10000.0
EPS = 1e-6
PAST_LEN = 1024
LAM_INIT = 0.8 - 0.6 * math.exp(-0.3 * 0)

LANES = 128
SUBLANES = 8
IN_COLS = 1024
N_IN_BLOCKS = 8
VMEM_LIMIT = 56 * 1024 * 1024

f32 = jnp.float32
bf16 = jnp.bfloat16

NT_DIMS = (((1,), (1,)), ((), ()))
TN_DIMS = (((0,), (0,)), ((), ()))


def _rms(x):
    return x * lax.rsqrt(jnp.mean(x * x, axis=-1, keepdims=True) + EPS)


def _rope(x, cos, sin_signed):
    return x * cos + pltpu.roll(x, RET_DK // 2, 1) * sin_signed


def _sigmoid(x):
    return 1.0 / (1.0 + jnp.exp(-x))


def _params(sem):
    return pltpu.CompilerParams(dimension_semantics=sem, vmem_limit_bytes=VMEM_LIMIT)


def _const_spec(shape):
    nd = len(shape)
    return pl.BlockSpec(shape, lambda *_: (0,) * nd)


def _in_proj_kernel(x_ref, gmix_ref, w_ref, cos_ref, sin_ref, gq_ref, gk_ref,
                    rqk_ref, rv_ref, rg_ref, dq_ref, dk_ref, dkb_ref, dv_ref, dvb_ref,
                    ga_ref, gb_ref, h_ref):
    j = pl.program_id(1)

    @pl.when(j == 0)
    def _():
        h_ref[...] = (_rms(x_ref[...]) * gmix_ref[...]).astype(bf16)

    z = jnp.dot(h_ref[...], w_ref[...], preferred_element_type=f32)
    cos = cos_ref[...]
    sin = sin_ref[...]
    n_heads = IN_COLS // LANES

    def head(g):
        return z[:, g * LANES:(g + 1) * LANES]

    @pl.when(j == 0)
    def _():
        for g in range(n_heads):
            r = _rope(head(g), cos, sin)
            if g >= RET_HEADS:
                r = r * (RET_DK ** -0.5)
            rqk_ref[:, g * LANES:(g + 1) * LANES] = r.astype(bf16)

    @pl.when(j == 1)
    def _():
        rv_ref[...] = z.astype(bf16)

    @pl.when(j == 2)
    def _():
        rg_ref[...] = (z * _sigmoid(z)).astype(bf16)

    @pl.when(j == 3)
    def _():
        for g in range(n_heads):
            r = _rope(_rms(head(g)) * gq_ref[...], cos, sin) * (DA_DK ** -0.5)
            dq_ref[:, g * LANES:(g + 1) * LANES] = r.astype(bf16)

    @pl.when(j == 4)
    def _():
        for g in range(n_heads):
            r = _rope(_rms(head(g)) * gk_ref[...], cos, sin)
            dk_ref[:, g * LANES:(g + 1) * LANES] = r
            dkb_ref[:, g * LANES:(g + 1) * LANES] = r.astype(bf16)

    @pl.when(j == 5)
    def _():
        dv_ref[...] = z
        dvb_ref[...] = z.astype(bf16)

    @pl.when(j == 6)
    def _():
        ga_ref[...] = _sigmoid(z).astype(bf16)

    @pl.when(j == 7)
    def _():
        gb_ref[...] = _sigmoid(z).astype(bf16)


def _in_proj(x2d, g_mix, w_in, cos, sin, g_qn, g_kn, *, tm):
    n = x2d.shape[0]
    pos_blocks = cos.shape[0] // tm
    row = pl.BlockSpec((tm, IN_COLS), lambda i, j: (i, 0))
    tab = pl.BlockSpec((tm, LANES), lambda i, j: (i % pos_blocks, 0))
    bf = jax.ShapeDtypeStruct((n, IN_COLS), bf16)
    fl = jax.ShapeDtypeStruct((n, IN_COLS), f32)
    return pl.pallas_call(
        _in_proj_kernel,
        grid=(n // tm, N_IN_BLOCKS),
        in_specs=[row, _const_spec((1, D_MODEL)),
                  pl.BlockSpec((D_MODEL, IN_COLS), lambda i, j: (0, j)),
                  tab, tab, _const_spec((1, LANES)), _const_spec((1, LANES))],
        out_specs=[row] * 10,
        out_shape=[bf, bf, bf, bf, fl, bf, fl, bf, bf, bf],
        scratch_shapes=[pltpu.VMEM((tm, D_MODEL), bf16)],
        compiler_params=_params(("parallel", "arbitrary")),
        name="in_proj",
    )(x2d, g_mix, w_in, cos, sin, g_qn, g_kn)


def _ret_log_gamma():
    return np.log(1.0 - np.exp2(-5.0 - np.arange(RET_HEADS, dtype=np.float64)))


def _ret_tables(c):
    log_g = jnp.asarray(_ret_log_gamma(), f32)[:, None, None]
    idx = jnp.arange(c, dtype=f32)
    diff = idx[:, None] - idx[None, :]
    decay = jnp.where(diff >= 0, jnp.exp(log_g * jnp.maximum(diff, 0.0)), 0.0)
    qd = jnp.broadcast_to(jnp.exp(log_g * (idx + 1.0)[None, :, None]), (RET_HEADS, c, RET_DV))
    kd = jnp.broadcast_to(jnp.exp(log_g * (c - 1.0 - idx)[None, :, None]), (RET_HEADS, c, RET_DK))
    return decay, qd, kd


def _retention_kernel(qk_ref, v_ref, g_ref, dec_ref, qd_ref, kd_ref, s0_ref, o_ref, s_ref, *, chunk):
    @pl.when(pl.program_id(1) == 0)
    def _():
        s_ref[...] = s0_ref[...]

    log_g = _ret_log_gamma()
    for h in range(RET_HEADS):
        q = qk_ref[:, h * RET_DK:(h + 1) * RET_DK]
        k = qk_ref[:, (RET_HEADS + h) * RET_DK:(RET_HEADS + h + 1) * RET_DK]
        v = v_ref[:, h * RET_DV:(h + 1) * RET_DV]
        state = s_ref[0, h]
        scores = lax.dot_general(q, k, NT_DIMS, preferred_element_type=f32) * dec_ref[h]
        intra = jnp.dot(scores.astype(bf16), v, preferred_element_type=f32)
        inter = jnp.dot(q, state.astype(bf16), preferred_element_type=f32) * qd_ref[h]
        kw = (k.astype(f32) * kd_ref[h]).astype(bf16)
        s_ref[0, h] = float(np.exp(chunk * log_g[h])) * state + lax.dot_general(
            kw, v, TN_DIMS, preferred_element_type=f32)
        o = _rms(intra + inter)
        gate = g_ref[:, h * RET_DV:(h + 1) * RET_DV].astype(f32)
        o_ref[:, h * RET_DV:(h + 1) * RET_DV] = (o * gate).astype(bf16)


def _retention(rqk, rv, rg, s0, *, chunk, n_chunks, row_index):
    nb = s0.shape[0]
    decay, qd, kd = _ret_tables(chunk)
    blk = pl.BlockSpec((chunk, IN_COLS), row_index)
    st = pl.BlockSpec((1, RET_HEADS, RET_DK, RET_DV), lambda b, c: (b, 0, 0, 0))
    return pl.pallas_call(
        functools.partial(_retention_kernel, chunk=chunk),
        grid=(nb, n_chunks),
        in_specs=[blk, blk, blk, _const_spec(decay.shape), _const_spec(qd.shape),
                  _const_spec(kd.shape), st],
        out_specs=[blk, st],
        out_shape=[jax.ShapeDtypeStruct(rqk.shape, bf16), jax.ShapeDtypeStruct(s0.shape, f32)],
        compiler_params=_params(("parallel", "arbitrary")),
        name="retention",
    )(rqk, rv, rg, decay, qd, kd, s0)


def _lam(lq1_ref, lk1_ref, lq2_ref, lk2_ref):
    e1 = jnp.exp(jnp.sum(lq1_ref[...] * lk1_ref[...], axis=-1, keepdims=True))
    e2 = jnp.exp(jnp.sum(lq2_ref[...] * lk2_ref[...], axis=-1, keepdims=True))
    return e1 - e2 + LAM_INIT


def _lane_tile(x, width):
    return jnp.concatenate([x] * (width // LANES), axis=1)


def _attn_prompt_kernel(q_ref, k_ref, v_ref, lq1_ref, lk1_ref, lq2_ref, lk2_ref, gd_ref,
                        o_ref, m_ref, l_ref, acc_ref, *, blk):
    qi = pl.program_id(2)
    m_ref[...] = jnp.full(m_ref.shape, -1e30, f32)
    l_ref[...] = jnp.zeros(l_ref.shape, f32)
    acc_ref[...] = jnp.zeros(acc_ref.shape, f32)

    def step(kb, masked):
        start = pl.multiple_of(kb * blk, blk)
        v = v_ref[pl.ds(start, blk), :]
        for a in range(2):
            q = q_ref[:, a * DA_DK:(a + 1) * DA_DK]
            k = k_ref[pl.ds(start, blk), a * DA_DK:(a + 1) * DA_DK]
            s = lax.dot_general(q, k, NT_DIMS, preferred_element_type=f32)
            if masked:
                row = lax.broadcasted_iota(jnp.int32, s.shape, 0) // CHUNK
                col = lax.broadcasted_iota(jnp.int32, s.shape, 1) // CHUNK
                s = jnp.where(col <= row, s, -1e30)
            m_prev = m_ref[a]
            m_new = jnp.maximum(m_prev, jnp.max(s, axis=1, keepdims=True))
            alpha = jnp.exp(m_prev - m_new)
            p = jnp.exp(s - _lane_tile(m_new, blk))
            l_ref[a] = alpha * l_ref[a] + jnp.sum(p, axis=1, keepdims=True)
            acc_ref[a] = acc_ref[a] * _lane_tile(alpha, DA_DV) + jnp.dot(
                p.astype(bf16), v, preferred_element_type=f32)
            m_ref[a] = m_new

    def body(kb, carry):
        step(kb, False)
        return carry

    lax.fori_loop(0, qi, body, 0)
    step(qi, True)

    lam = _lam(lq1_ref, lk1_ref, lq2_ref, lk2_ref)
    o0 = acc_ref[0] * _lane_tile(1.0 / l_ref[0], DA_DV)
    o1 = acc_ref[1] * _lane_tile(1.0 / l_ref[1], DA_DV)
    o = _rms(o0 - lam * o1) * gd_ref[...] * (1.0 - LAM_INIT)
    o_ref[...] = o.astype(bf16)


def _attn_prompt(dq, dkb, dvb, lam_params, g_dnorm, *, nb, seq, blk):
    hw = 2 * DA_DK
    qspec = pl.BlockSpec((blk, hw), lambda b, h, i: (b * (seq // blk) + i, h))
    kvspec = pl.BlockSpec((seq, hw), lambda b, h, i: (b, h))
    vec = _const_spec((1, DA_DK))
    return pl.pallas_call(
        functools.partial(_attn_prompt_kernel, blk=blk),
        grid=(nb, DA_HEADS, seq // blk),
        in_specs=[qspec, kvspec, kvspec, vec, vec, vec, vec, _const_spec((1, DA_DV))],
        out_specs=qspec,
        out_shape=jax.ShapeDtypeStruct(dq.shape, bf16),
        scratch_shapes=[pltpu.VMEM((2, blk, LANES), f32), pltpu.VMEM((2, blk, LANES), f32),
                        pltpu.VMEM((2, blk, DA_DV), f32)],
        compiler_params=_params(("parallel", "parallel", "arbitrary")),
        name="attn_prompt",
    )(dq, dkb, dvb, *lam_params, g_dnorm)


def _attn_decode_kernel(q_ref, kc_ref, vc_ref, kn_ref, vn_ref, lq1_ref, lk1_ref, lq2_ref, lk2_ref,
                        gd_ref, o_ref):
    probs = []
    for a in range(2):
        q = q_ref[:, a * DA_DK:(a + 1) * DA_DK]
        kc = kc_ref[:, a * DA_DK:(a + 1) * DA_DK].astype(bf16)
        kn = kn_ref[:, a * DA_DK:(a + 1) * DA_DK]
        sc = lax.dot_general(q, kc, NT_DIMS, preferred_element_type=f32)
        sn = lax.dot_general(q, kn, NT_DIMS, preferred_element_type=f32)
        m = jnp.maximum(jnp.max(sc, axis=1, keepdims=True), jnp.max(sn, axis=1, keepdims=True))
        pc = jnp.exp(sc - m)
        pn = jnp.exp(sn - m)
        inv = 1.0 / (jnp.sum(pc, axis=1, keepdims=True) + jnp.sum(pn, axis=1, keepdims=True))
        probs.append((pc * inv, pn * inv))
    lam = _lam(lq1_ref, lk1_ref, lq2_ref, lk2_ref)
    ac = probs[0][0] - lam * probs[1][0]
    an = probs[0][1] - lam * probs[1][1]
    o = jnp.dot(ac.astype(bf16), vc_ref[...].astype(bf16), preferred_element_type=f32)
    o = o + jnp.dot(an.astype(bf16), vn_ref[...], preferred_element_type=f32)
    o_ref[...] = (_rms(o) * gd_ref[...] * (1.0 - LAM_INIT)).astype(bf16)


def _attn_decode(dq_v, dkb_v, dvb_v, cache_k2d, cache_v2d, lam_params, g_dnorm, *, nb, t, past):
    hw = 2 * DA_DK
    new = pl.BlockSpec((t, hw), lambda b, h: (0, b * DA_HEADS + h))
    old = pl.BlockSpec((past, hw), lambda b, h: (b, h))
    vec = _const_spec((1, DA_DK))
    return pl.pallas_call(
        _attn_decode_kernel,
        grid=(nb, DA_HEADS),
        in_specs=[new, old, old, new, new, vec, vec, vec, vec, _const_spec((1, DA_DV))],
        out_specs=new,
        out_shape=jax.ShapeDtypeStruct(dq_v.shape, bf16),
        compiler_params=_params(("parallel", "parallel")),
        name="attn_decode",
    )(dq_v, cache_k2d, cache_v2d, dkb_v, dvb_v, *lam_params, g_dnorm)


def _merge_kernel(x_ref, oa_ref, ob_ref, ga_ref, gb_ref, wr_ref, wd_ref, wo_ref, y_ref):
    a = jnp.dot(oa_ref[...], wr_ref[...], preferred_element_type=f32)
    b = jnp.dot(ob_ref[...], wd_ref[...], preferred_element_type=f32)
    m = ga_ref[...].astype(f32) * a + gb_ref[...].astype(f32) * b
    y_ref[...] = x_ref[...] + jnp.dot(m.astype(bf16), wo_ref[...], preferred_element_type=f32)


def _merge(x2d, oa, ob, ga, gb, w_ret_br, w_da_br, w_out, *, tm):
    n = x2d.shape[0]
    row = pl.BlockSpec((tm, D_MODEL), lambda i: (i, 0))
    w = _const_spec((D_MODEL, D_MODEL))
    return pl.pallas_call(
        _merge_kernel,
        grid=(n // tm,),
        in_specs=[row] * 5 + [w] * 3,
        out_specs=row,
        out_shape=jax.ShapeDtypeStruct(x2d.shape, f32),
        compiler_params=_params(("parallel",)),
        name="merge",
    )(x2d, oa, ob, ga, gb, w_ret_br, w_da_br, w_out)


FF_CHUNK = 256


def _ffn_kernel(x_ref, ple_ref, cst_ref, gffn_ref, wup_ref, cw_ref, cb_ref, wdn_ref, gple_ref,
                wpg_ref, wple_ref, y_ref, cnew_ref, uext_ref, act_ref, *, tm, shift, halo):
    @pl.when(pl.program_id(1) == 0)
    def _():
        uext_ref[0:halo, :] = cst_ref[0]

    x1 = x_ref[...]
    xn = (_rms(x1) * gffn_ref[...]).astype(bf16)
    uext_ref[halo:halo + tm, :] = jnp.dot(xn, wup_ref[...], preferred_element_type=f32)

    def conv(col):
        sl = slice(col, col + FF_CHUNK)
        acc = cw_ref[0:1, sl] * uext_ref[halo - 2 * shift:halo - 2 * shift + tm, sl]
        acc = acc + cw_ref[1:2, sl] * uext_ref[halo - shift:halo - shift + tm, sl]
        acc = acc + cw_ref[2:3, sl] * uext_ref[halo:halo + tm, sl]
        return acc + cb_ref[:, sl]

    for c in range(D_FF // FF_CHUNK):
        cg = conv(c * FF_CHUNK)
        cv = conv(D_FF + c * FF_CHUNK)
        act_ref[:, c * FF_CHUNK:(c + 1) * FF_CHUNK] = (cg * _sigmoid(cg) * cv).astype(bf16)

    x2 = x1 + jnp.dot(act_ref[...], wdn_ref[...], preferred_element_type=f32)
    tail = uext_ref[tm:tm + halo, :]
    cnew_ref[0] = tail
    uext_ref[0:halo, :] = tail

    gate = jnp.dot((_rms(x2) * gple_ref[...]).astype(bf16), wpg_ref[...], preferred_element_type=f32)
    emb = jnp.dot(ple_ref[...].astype(bf16), wple_ref[...], preferred_element_type=f32)
    y_ref[...] = x2 + _sigmoid(gate) * emb


def _ffn(x1, ple2d, cstate, g_ffn, w_up, conv_w, conv_b, w_down, g_ple, w_ple_gate, w_ple, *,
         tm, shift):
    n = x1.shape[0]
    nseg, halo, _ = cstate.shape
    tiles = n // (nseg * tm)
    ple_dim = ple2d.shape[1]

    def resident(shape):
        nd = len(shape)
        return pl.BlockSpec(shape, lambda *_: (0,) * nd, pipeline_mode=pl.Buffered(1))

    row = lambda width: pl.BlockSpec((tm, width), lambda s, t: (s * tiles + t, 0))
    seg = pl.BlockSpec((1, halo, 2 * D_FF), lambda s, t: (s, 0, 0))
    return pl.pallas_call(
        functools.partial(_ffn_kernel, tm=tm, shift=shift, halo=halo),
        grid=(nseg, tiles),
        in_specs=[row(D_MODEL), row(ple_dim), seg, resident((1, D_MODEL)),
                  resident((D_MODEL, 2 * D_FF)), resident((CONV_W, 2 * D_FF)),
                  resident((1, 2 * D_FF)), resident((D_FF, D_MODEL)), resident((1, D_MODEL)),
                  resident((D_MODEL, D_MODEL)), resident((ple_dim, D_MODEL))],
        out_specs=[row(D_MODEL), seg],
        out_shape=[jax.ShapeDtypeStruct(x1.shape, f32), jax.ShapeDtypeStruct(cstate.shape, f32)],
        scratch_shapes=[pltpu.VMEM((halo + tm, 2 * D_FF), f32), pltpu.VMEM((tm, D_FF), bf16)],
        compiler_params=_params(("parallel", "arbitrary")),
        name="ffn",
    )(x1, ple2d, cstate, g_ffn, w_up, conv_w, conv_b, w_down, g_ple, w_ple_gate, w_ple)


def _rope_tables(pos):
    inv = ROPE_THETA ** (-jnp.arange(0, RET_DK, 2, dtype=f32) / RET_DK)
    ang = pos.astype(f32)[:, None] * inv[None, :]
    ang = jnp.concatenate([ang, ang], axis=-1)
    sign = jnp.where(jnp.arange(RET_DK) < RET_DK // 2, -1.0, 1.0).astype(f32)
    return jnp.cos(ang), jnp.sin(ang) * sign


def _layer_prompt(x, ple, w, *, tm_in=512, tm_merge=512, tm_ffn=256, ret_chunk=256, attn_blk=256):
    nb, seq, _ = x.shape
    n = nb * seq
    tm_in, tm_merge, tm_ffn = min(tm_in, seq), min(tm_merge, seq), min(tm_ffn, seq)
    ret_chunk, attn_blk = min(ret_chunk, seq), min(attn_blk, seq)
    x2d = x.reshape(n, D_MODEL)
    cos, sin = _rope_tables(jnp.arange(seq))
    rqk, rv, rg, dq, dk, dkb, dv, dvb, ga, gb = _in_proj(
        x2d, w["g_mix"], w["w_in"], cos, sin, w["g_qn"], w["g_kn"], tm=tm_in)

    n_chunks = seq // ret_chunk
    s0 = jnp.zeros((nb, RET_HEADS, RET_DK, RET_DV), f32)
    oa, s_new = _retention(rqk, rv, rg, s0, chunk=ret_chunk, n_chunks=n_chunks,
                           row_index=lambda b, c: (b * n_chunks + c, 0))
    ob = _attn_prompt(dq, dkb, dvb, w["lam"], w["g_dnorm"], nb=nb, seq=seq, blk=attn_blk)
    x1 = _merge(x2d, oa, ob, ga, gb, w["w_ret_br"], w["w_da_br"], w["w_out"], tm=tm_merge)

    cstate = jnp.zeros((nb, SUBLANES, 2 * D_FF), f32)
    y, ctail = _ffn(x1, ple.reshape(n, -1), cstate, w["g_ffn"], w["w_up"], w["conv_w"], w["conv_b"],
                    w["w_down"], w["g_ple"], w["w_ple_gate"], w["w_ple"], tm=tm_ffn, shift=1)
    return (y.reshape(nb, seq, D_MODEL),
            dk.reshape(nb, seq, DA_HEADS, 2, DA_DK), dv.reshape(nb, seq, DA_HEADS, DA_DV),
            s_new, ctail[:, SUBLANES - (CONV_W - 1):])


def _layer_decode(x, ple, cache_k, cache_v, state_ret, state_conv, w, *, past_len):
    nb, t, _ = x.shape
    n = nb * t
    past = cache_k.shape[1]
    tm = n

    def to_tm(a):
        return jnp.swapaxes(a, 0, 1).reshape(n, a.shape[-1])

    def from_tm(a):
        return jnp.swapaxes(a.reshape(t, nb, a.shape[-1]), 0, 1)

    cos, sin = _rope_tables(past_len + jnp.arange(n) // nb)
    rqk, rv, rg, dq, dk, dkb, dv, dvb, ga, gb = _in_proj(
        to_tm(x), w["g_mix"], w["w_in"], cos, sin, w["g_qn"], w["g_kn"], tm=tm)

    view = lambda a: a.reshape(t, nb * IN_COLS)
    oa, s_new = _retention(view(rqk), view(rv), view(rg), state_ret, chunk=t, n_chunks=1,
                           row_index=lambda b, c: (0, b))
    ob = _attn_decode(view(dq), view(dkb), view(dvb),
                      cache_k.reshape(nb * past, DA_HEADS * 2 * DA_DK),
                      cache_v.reshape(nb * past, DA_HEADS * DA_DV),
                      w["lam"], w["g_dnorm"], nb=nb, t=t, past=past)
    x1 = _merge(to_tm(x), oa.reshape(n, IN_COLS), ob.reshape(n, IN_COLS), ga, gb,
                w["w_ret_br"], w["w_da_br"], w["w_out"], tm=tm)

    halo = (CONV_W - 1) * nb
    cstate = jnp.swapaxes(state_conv, 0, 1).reshape(1, halo, 2 * D_FF)
    y, ctail = _ffn(x1, to_tm(ple), cstate, w["g_ffn"], w["w_up"], w["conv_w"], w["conv_b"],
                    w["w_down"], w["g_ple"], w["w_ple_gate"], w["w_ple"], tm=tm, shift=nb)
    return (from_tm(y),
            from_tm(dk).reshape(nb, t, DA_HEADS, 2, DA_DK), from_tm(dv).reshape(nb, t, DA_HEADS, DA_DV),
            s_new, jnp.swapaxes(ctail.reshape(CONV_W - 1, nb, 2 * D_FF), 0, 1))


def _layer_weights(i, g_mix, w_in, g_qn, g_kn, lam_q1, lam_k1, lam_q2, lam_k2, g_dnorm, w_ret_br,
                   w_da_br, w_out, g_ffn, w_up, conv_w, conv_b, w_down, g_ple, w_ple_gate, w_ple):
    vec = lambda a: a[i].reshape(1, -1)
    return dict(
        g_mix=vec(g_mix), w_in=w_in[i].astype(bf16), g_qn=vec(g_qn), g_kn=vec(g_kn),
        lam=(vec(lam_q1), vec(lam_k1), vec(lam_q2), vec(lam_k2)), g_dnorm=vec(g_dnorm),
        w_ret_br=w_ret_br[i].astype(bf16), w_da_br=w_da_br[i].astype(bf16),
        w_out=w_out[i].astype(bf16), g_ffn=vec(g_ffn), w_up=w_up[i].astype(bf16),
        conv_w=conv_w[i], conv_b=vec(conv_b), w_down=w_down[i].astype(bf16), g_ple=vec(g_ple),
        w_ple_gate=w_ple_gate[i].astype(bf16), w_ple=w_ple[i].astype(bf16))


def kernel(x_prompt, x_sample, cache_k, cache_v, state_ret, state_conv, p_prompt, p_sample, g_mix, w_in, g_qn, g_kn, lam_q1, lam_k1, lam_q2, lam_k2, g_dnorm, w_ret_br, w_da_br, w_out, g_ffn, w_up, conv_w, conv_b, w_down, g_ple, w_ple_gate, w_ple):
    depth = w_in.shape[0]
    yp, ys = x_prompt, x_sample
    outs = [[] for _ in range(8)]
    for i in range(depth):
        w = _layer_weights(i, g_mix, w_in, g_qn, g_kn, lam_q1, lam_k1, lam_q2, lam_k2, g_dnorm,
                           w_ret_br, w_da_br, w_out, g_ffn, w_up, conv_w, conv_b, w_down, g_ple,
                           w_ple_gate, w_ple)
        yp, k1, v1, s1, c1 = _layer_prompt(yp, p_prompt[i], w)
        ys, k2, v2, s2, c2 = _layer_decode(ys, p_sample[i], cache_k[i], cache_v[i], state_ret[i],
                                           state_conv[i], w, past_len=PAST_LEN)
        for lst, val in zip(outs, (k1, k2, v1, v2, s1, s2, c1, c2)):
            lst.append(val)
    return (yp, ys) + tuple(jnp.stack(o) for o in outs)
```

```python
import functools
import math

import jax
import jax.numpy as jnp
import numpy as np
from jax import lax
from jax.experimental import pallas as pl
from jax.experimental.pallas import tpu as pltpu

D_MODEL = 1024
CHUNK = 64
RET_HEADS = 4
RET_DK = 128
RET_DV = 256
DA_HEADS = 4
DA_DK = 128
DA_DV = 256
D_FF = 2816
CONV_W = 3
ROPE_THETA = 10000.0
EPS = 1e-6
PAST_LEN = 1024
LAM_INIT = 0.8 - 0.6 * math.exp(-0.3 * 0)
LOG2E = math.log2(math.e)

LANES = 128
SUBLANES = 8
IN_COLS = 1024
N_IN_BLOCKS = 8
VMEM_LIMIT = 56 * 1024 * 1024

f32 = jnp.float32
bf16 = jnp.bfloat16

NT_DIMS = (((1,), (1,)), ((), ()))
TN_DIMS = (((0,), (0,)), ((), ()))


def _rms(x):
    return x * lax.rsqrt(jnp.mean(x * x, axis=-1, keepdims=True) + EPS)


def _rope(x, cos, sin_signed):
    return x * cos + pltpu.roll(x, RET_DK // 2, 1) * sin_signed


def _sigmoid(x):
    return 0.5 * jnp.tanh(0.5 * x) + 0.5


def _params(sem):
    return pltpu.CompilerParams(dimension_semantics=sem, vmem_limit_bytes=VMEM_LIMIT)


def _const_spec(shape):
    nd = len(shape)
    return pl.BlockSpec(shape, lambda *_: (0,) * nd)


def _in_proj_kernel(x_ref, gmix_ref, w_ref, cos_ref, sin_ref, gq_ref, gk_ref,
                    rqk_ref, rv_ref, rg_ref, dq_ref, dk_ref, dkb_ref, dv_ref, dvb_ref,
                    ga_ref, gb_ref):
    h = (_rms(x_ref[...]) * gmix_ref[...]).astype(bf16)
    cos = cos_ref[...]
    sin = sin_ref[...]
    n_heads = IN_COLS // LANES

    def block(j):
        return jnp.dot(h, w_ref[:, j * IN_COLS:(j + 1) * IN_COLS], preferred_element_type=f32)

    def heads(z):
        return [(slice(g * LANES, (g + 1) * LANES), z[:, g * LANES:(g + 1) * LANES])
                for g in range(n_heads)]

    for g, (sl, zg) in enumerate(heads(block(0))):
        r = _rope(zg, cos, sin)
        if g >= RET_HEADS:
            r = r * (RET_DK ** -0.5)
        rqk_ref[:, sl] = r.astype(bf16)

    rv_ref[...] = block(1).astype(bf16)
    z = block(2)
    rg_ref[...] = (z * _sigmoid(z)).astype(bf16)

    for sl, zg in heads(block(3)):
        r = _rope(_rms(zg) * gq_ref[...], cos, sin) * (DA_DK ** -0.5 * LOG2E)
        dq_ref[:, sl] = r.astype(bf16)

    for sl, zg in heads(block(4)):
        r = _rope(_rms(zg) * gk_ref[...], cos, sin)
        dk_ref[:, sl] = r
        dkb_ref[:, sl] = r.astype(bf16)

    z = block(5)
    dv_ref[...] = z
    dvb_ref[...] = z.astype(bf16)
    ga_ref[...] = _sigmoid(block(6)).astype(bf16)
    gb_ref[...] = _sigmoid(block(7)).astype(bf16)


def _resident(shape):
    nd = len(shape)
    return pl.BlockSpec(shape, lambda *_: (0,) * nd, pipeline_mode=pl.Buffered(1))


def _in_proj(x2d, g_mix, w_in, cos, sin, g_qn, g_kn, *, tm):
    n = x2d.shape[0]
    pos_blocks = cos.shape[0] // tm
    row = pl.BlockSpec((tm, IN_COLS), lambda i: (i, 0))
    tab = pl.BlockSpec((tm, LANES), lambda i: (i % pos_blocks, 0))
    bf = jax.ShapeDtypeStruct((n, IN_COLS), bf16)
    fl = jax.ShapeDtypeStruct((n, IN_COLS), f32)
    return pl.pallas_call(
        _in_proj_kernel,
        grid=(n // tm,),
        in_specs=[row, _resident((1, D_MODEL)), _resident((D_MODEL, N_IN_BLOCKS * IN_COLS)),
                  tab, tab, _resident((1, LANES)), _resident((1, LANES))],
        out_specs=[row] * 10,
        out_shape=[bf, bf, bf, bf, fl, bf, fl, bf, bf, bf],
        compiler_params=_params(("parallel",)),
        name="in_proj",
    )(x2d, g_mix, w_in, cos, sin, g_qn, g_kn)


def _ret_log_gamma():
    return np.log(1.0 - np.exp2(-5.0 - np.arange(RET_HEADS, dtype=np.float64)))


def _ret_tables(c):
    log_g = jnp.asarray(_ret_log_gamma(), f32)[:, None, None]
    idx = jnp.arange(c, dtype=f32)
    diff = idx[:, None] - idx[None, :]
    decay = jnp.where(diff >= 0, jnp.exp(log_g * jnp.maximum(diff, 0.0)), 0.0)
    qd = jnp.broadcast_to(jnp.exp(log_g * (idx + 1.0)[None, :, None]), (RET_HEADS, c, RET_DV))
    kd = jnp.broadcast_to(jnp.exp(log_g * (c - 1.0 - idx)[None, :, None]), (RET_HEADS, c, RET_DK))
    return decay, qd, kd


def _retention_kernel(qk_ref, v_ref, g_ref, dec_ref, qd_ref, kd_ref, s0_ref, o_ref, s_ref, *, chunk):
    @pl.when(pl.program_id(1) == 0)
    def _():
        s_ref[...] = s0_ref[...]

    log_g = _ret_log_gamma()
    for h in range(RET_HEADS):
        q = qk_ref[:, h * RET_DK:(h + 1) * RET_DK]
        k = qk_ref[:, (RET_HEADS + h) * RET_DK:(RET_HEADS + h + 1) * RET_DK]
        v = v_ref[:, h * RET_DV:(h + 1) * RET_DV]
        state = s_ref[0, h]
        scores = lax.dot_general(q, k, NT_DIMS, preferred_element_type=f32) * dec_ref[h]
        intra = jnp.dot(scores.astype(bf16), v, preferred_element_type=f32)
        inter = jnp.dot(q, state.astype(bf16), preferred_element_type=f32) * qd_ref[h]
        kw = (k.astype(f32) * kd_ref[h]).astype(bf16)
        s_ref[0, h] = float(np.exp(chunk * log_g[h])) * state + lax.dot_general(
            kw, v, TN_DIMS, preferred_element_type=f32)
        o = _rms(intra + inter)
        gate = g_ref[:, h * RET_DV:(h + 1) * RET_DV].astype(f32)
        o_ref[:, h * RET_DV:(h + 1) * RET_DV] = (o * gate).astype(bf16)


def _retention(rqk, rv, rg, s0, *, chunk, n_chunks, row_index):
    nb = s0.shape[0]
    decay, qd, kd = _ret_tables(chunk)
    blk = pl.BlockSpec((chunk, IN_COLS), row_index)
    st = pl.BlockSpec((1, RET_HEADS, RET_DK, RET_DV), lambda b, c: (b, 0, 0, 0))
    return pl.pallas_call(
        functools.partial(_retention_kernel, chunk=chunk),
        grid=(nb, n_chunks),
        in_specs=[blk, blk, blk, _const_spec(decay.shape), _const_spec(qd.shape),
                  _const_spec(kd.shape), st],
        out_specs=[blk, st],
        out_shape=[jax.ShapeDtypeStruct(rqk.shape, bf16), jax.ShapeDtypeStruct(s0.shape, f32)],
        compiler_params=_params(("parallel", "arbitrary")),
        name="retention",
    )(rqk, rv, rg, decay, qd, kd, s0)


def _lam(lq1_ref, lk1_ref, lq2_ref, lk2_ref):
    e1 = jnp.exp(jnp.sum(lq1_ref[...] * lk1_ref[...], axis=-1, keepdims=True))
    e2 = jnp.exp(jnp.sum(lq2_ref[...] * lk2_ref[...], axis=-1, keepdims=True))
    return e1 - e2 + LAM_INIT


ATTN_SUB = 64


def _lane_tile(x, width):
    return jnp.concatenate([x] * (width // LANES), axis=1)


def _attn_prompt_kernel(q_ref, k_ref, v_ref, lq1_ref, lk1_ref, lq2_ref, lk2_ref, gd_ref,
                        o_ref, m_ref, l_ref, acc_ref, sa_ref, sb_ref, alpha_ref, p_ref, *, blk):
    qi = pl.program_id(2)
    m_ref[...] = jnp.full(m_ref.shape, -1e30, f32)
    l_ref[...] = jnp.zeros(l_ref.shape, f32)
    acc_ref[...] = jnp.zeros(acc_ref.shape, f32)

    def scores(kb, s_ref):
        start = pl.multiple_of(kb * blk, blk)
        for a in range(2):
            q = q_ref[:, a * DA_DK:(a + 1) * DA_DK]
            k = k_ref[pl.ds(start, blk), a * DA_DK:(a + 1) * DA_DK]
            s_ref[a] = lax.dot_general(q, k, NT_DIMS, preferred_element_type=f32)

    def update(kb, slot, masked):
        s_ref = (sa_ref, sb_ref)[slot]
        start = pl.multiple_of(kb * blk, blk)
        v = v_ref[pl.ds(start, blk), :]
        for a in range(2):
            buf = 2 * slot + a
            for r in range(blk // ATTN_SUB):
                rows = slice(r * ATTN_SUB, (r + 1) * ATTN_SUB)
                s = s_ref[a, rows, :]
                if masked:
                    row = (lax.broadcasted_iota(jnp.int32, s.shape, 0) + r * ATTN_SUB) // CHUNK
                    col = lax.broadcasted_iota(jnp.int32, s.shape, 1) // CHUNK
                    s = jnp.where(col <= row, s, -1e30)
                m_prev = m_ref[a, rows, :]
                m_new = jnp.maximum(m_prev, jnp.max(s, axis=1, keepdims=True))
                alpha = jnp.exp2(m_prev - m_new)
                p = jnp.exp2(s - _lane_tile(m_new, blk))
                l_ref[a, rows, :] = alpha * l_ref[a, rows, :] + jnp.sum(p, axis=1, keepdims=True)
                m_ref[a, rows, :] = m_new
                alpha_ref[buf, rows, :] = alpha
                p_ref[buf, rows, :] = p.astype(bf16)
            acc_ref[a] = acc_ref[a] * _lane_tile(alpha_ref[buf], DA_DV) + jnp.dot(
                p_ref[buf], v, preferred_element_type=f32)

    scores(0, sa_ref)

    def body(i, carry):
        kb = 2 * i
        scores(kb + 1, sb_ref)
        update(kb, 0, False)
        scores(kb + 2, sa_ref)
        update(kb + 1, 1, False)
        return carry

    lax.fori_loop(0, lax.shift_right_logical(qi, 1), body, 0)

    @pl.when(lax.bitwise_and(qi, 1) == 1)
    def _():
        scores(qi, sb_ref)
        update(qi - 1, 0, False)
        update(qi, 1, True)

    @pl.when(lax.bitwise_and(qi, 1) == 0)
    def _():
        update(qi, 0, True)

    lam = _lam(lq1_ref, lk1_ref, lq2_ref, lk2_ref)
    o0 = acc_ref[0] * _lane_tile(1.0 / l_ref[0], DA_DV)
    o1 = acc_ref[1] * _lane_tile(1.0 / l_ref[1], DA_DV)
    o = _rms(o0 - lam * o1) * gd_ref[...] * (1.0 - LAM_INIT)
    o_ref[...] = o.astype(bf16)


def _attn_prompt(dq, dkb, dvb, lam_params, g_dnorm, *, nb, seq, blk):
    hw = 2 * DA_DK
    qspec = pl.BlockSpec((blk, hw), lambda b, h, i: (b * (seq // blk) + i, h))
    kvspec = pl.BlockSpec((seq, hw), lambda b, h, i: (b, h))
    vec = _const_spec((1, DA_DK))
    return pl.pallas_call(
        functools.partial(_attn_prompt_kernel, blk=blk),
        grid=(nb, DA_HEADS, seq // blk),
        in_specs=[qspec, kvspec, kvspec, vec, vec, vec, vec, _const_spec((1, DA_DV))],
        out_specs=qspec,
        out_shape=jax.ShapeDtypeStruct(dq.shape, bf16),
        scratch_shapes=[pltpu.VMEM((2, blk, LANES), f32), pltpu.VMEM((2, blk, LANES), f32),
                        pltpu.VMEM((2, blk, DA_DV), f32), pltpu.VMEM((2, blk, blk), f32),
                        pltpu.VMEM((2, blk, blk), f32), pltpu.VMEM((4, blk, LANES), f32),
                        pltpu.VMEM((4, blk, blk), bf16)],
        compiler_params=_params(("parallel", "parallel", "arbitrary")),
        name="attn_prompt",
    )(dq, dkb, dvb, *lam_params, g_dnorm)


def _attn_decode_kernel(q_ref, kc_ref, vc_ref, kn_ref, vn_ref, lq1_ref, lk1_ref, lq2_ref, lk2_ref,
                        gd_ref, o_ref):
    probs = []
    for a in range(2):
        q = q_ref[:, a * DA_DK:(a + 1) * DA_DK]
        kc = kc_ref[:, a * DA_DK:(a + 1) * DA_DK].astype(bf16)
        kn = kn_ref[:, a * DA_DK:(a + 1) * DA_DK]
        sc = lax.dot_general(q, kc, NT_DIMS, preferred_element_type=f32)
        sn = lax.dot_general(q, kn, NT_DIMS, preferred_element_type=f32)
        m = jnp.maximum(jnp.max(sc, axis=1, keepdims=True), jnp.max(sn, axis=1, keepdims=True))
        pc = jnp.exp2(sc - m)
        pn = jnp.exp2(sn - m)
        inv = 1.0 / (jnp.sum(pc, axis=1, keepdims=True) + jnp.sum(pn, axis=1, keepdims=True))
        probs.append((pc * inv, pn * inv))
    lam = _lam(lq1_ref, lk1_ref, lq2_ref, lk2_ref)
    ac = probs[0][0] - lam * probs[1][0]
    an = probs[0][1] - lam * probs[1][1]
    o = jnp.dot(ac.astype(bf16), vc_ref[...].astype(bf16), preferred_element_type=f32)
    o = o + jnp.dot(an.astype(bf16), vn_ref[...], preferred_element_type=f32)
    o_ref[...] = (_rms(o) * gd_ref[...] * (1.0 - LAM_INIT)).astype(bf16)


def _attn_decode(dq_v, dkb_v, dvb_v, cache_k2d, cache_v2d, lam_params, g_dnorm, *, nb, t, past):
    hw = 2 * DA_DK
    new = pl.BlockSpec((t, hw), lambda b, h: (0, b * DA_HEADS + h))
    old = pl.BlockSpec((past, hw), lambda b, h: (b, h))
    vec = _const_spec((1, DA_DK))
    return pl.pallas_call(
        _attn_decode_kernel,
        grid=(nb, DA_HEADS),
        in_specs=[new, old, old, new, new, vec, vec, vec, vec, _const_spec((1, DA_DV))],
        out_specs=new,
        out_shape=jax.ShapeDtypeStruct(dq_v.shape, bf16),
        compiler_params=_params(("parallel", "parallel")),
        name="attn_decode",
    )(dq_v, cache_k2d, cache_v2d, dkb_v, dvb_v, *lam_params, g_dnorm)


def _merge_kernel(x_ref, oa_ref, ob_ref, ga_ref, gb_ref, wr_ref, wd_ref, wo_ref, y_ref):
    a = jnp.dot(oa_ref[...], wr_ref[...], preferred_element_type=f32)
    b = jnp.dot(ob_ref[...], wd_ref[...], preferred_element_type=f32)
    m = ga_ref[...].astype(f32) * a + gb_ref[...].astype(f32) * b
    y_ref[...] = x_ref[...] + jnp.dot(m.astype(bf16), wo_ref[...], preferred_element_type=f32)


def _merge(x2d, oa, ob, ga, gb, w_ret_br, w_da_br, w_out, *, tm):
    n = x2d.shape[0]
    row = pl.BlockSpec((tm, D_MODEL), lambda i: (i, 0))
    w = _const_spec((D_MODEL, D_MODEL))
    return pl.pallas_call(
        _merge_kernel,
        grid=(n // tm,),
        in_specs=[row] * 5 + [w] * 3,
        out_specs=row,
        out_shape=jax.ShapeDtypeStruct(x2d.shape, f32),
        compiler_params=_params(("parallel",)),
        name="merge",
    )(x2d, oa, ob, ga, gb, w_ret_br, w_da_br, w_out)


FF_CHUNK = 256


def _ffn_kernel(x_ref, ple_ref, cst_ref, gffn_ref, wup_ref, cw_ref, cb_ref, wdn_ref, gple_ref,
                wpg_ref, wple_ref, y_ref, cnew_ref, uext_ref, act_ref, *, tm, shift, halo):
    @pl.when(pl.program_id(1) == 0)
    def _():
        uext_ref[0:halo, :] = cst_ref[0]

    x1 = x_ref[...]
    xn = (_rms(x1) * gffn_ref[...]).astype(bf16)
    uext_ref[halo:halo + tm, :] = jnp.dot(xn, wup_ref[...], preferred_element_type=f32)

    def conv(col):
        sl = slice(col, col + FF_CHUNK)
        acc = cw_ref[0:1, sl] * uext_ref[halo - 2 * shift:halo - 2 * shift + tm, sl]
        acc = acc + cw_ref[1:2, sl] * uext_ref[halo - shift:halo - shift + tm, sl]
        acc = acc + cw_ref[2:3, sl] * uext_ref[halo:halo + tm, sl]
        return acc + cb_ref[:, sl]

    for c in range(D_FF // FF_CHUNK):
        cg = conv(c * FF_CHUNK)
        cv = conv(D_FF + c * FF_CHUNK)
        act_ref[:, c * FF_CHUNK:(c + 1) * FF_CHUNK] = (cg * _sigmoid(cg) * cv).astype(bf16)

    x2 = x1 + jnp.dot(act_ref[...], wdn_ref[...], preferred_element_type=f32)
    tail = uext_ref[tm:tm + halo, :]
    cnew_ref[0] = tail
    uext_ref[0:halo, :] = tail

    gate = jnp.dot((_rms(x2) * gple_ref[...]).astype(bf16), wpg_ref[...], preferred_element_type=f32)
    emb = jnp.dot(ple_ref[...].astype(bf16), wple_ref[...], preferred_element_type=f32)
    y_ref[...] = x2 + _sigmoid(gate) * emb


def _ffn(x1, ple2d, cstate, g_ffn, w_up, conv_w, conv_b, w_down, g_ple, w_ple_gate, w_ple, *,
         tm, shift):
    n = x1.shape[0]
    nseg, halo, _ = cstate.shape
    tiles = n // (nseg * tm)
    ple_dim = ple2d.shape[1]
    resident = _resident
    row = lambda width: pl.BlockSpec((tm, width), lambda s, t: (s * tiles + t, 0))
    seg = pl.BlockSpec((1, halo, 2 * D_FF), lambda s, t: (s, 0, 0))
    return pl.pallas_call(
        functools.partial(_ffn_kernel, tm=tm, shift=shift, halo=halo),
        grid=(nseg, tiles),
        in_specs=[row(D_MODEL), row(ple_dim), seg, resident((1, D_MODEL)),
                  resident((D_MODEL, 2 * D_FF)), resident((CONV_W, 2 * D_FF)),
                  resident((1, 2 * D_FF)), resident((D_FF, D_MODEL)), resident((1, D_MODEL)),
                  resident((D_MODEL, D_MODEL)), resident((ple_dim, D_MODEL))],
        out_specs=[row(D_MODEL), seg],
        out_shape=[jax.ShapeDtypeStruct(x1.shape, f32), jax.ShapeDtypeStruct(cstate.shape, f32)],
        scratch_shapes=[pltpu.VMEM((halo + tm, 2 * D_FF), f32), pltpu.VMEM((tm, D_FF), bf16)],
        compiler_params=_params(("parallel", "arbitrary")),
        name="ffn",
    )(x1, ple2d, cstate, g_ffn, w_up, conv_w, conv_b, w_down, g_ple, w_ple_gate, w_ple)


def _rope_tables(pos):
    inv = ROPE_THETA ** (-jnp.arange(0, RET_DK, 2, dtype=f32) / RET_DK)
    ang = pos.astype(f32)[:, None] * inv[None, :]
    ang = jnp.concatenate([ang, ang], axis=-1)
    sign = jnp.where(jnp.arange(RET_DK) < RET_DK // 2, -1.0, 1.0).astype(f32)
    return jnp.cos(ang), jnp.sin(ang) * sign


def _layer_prompt(x, ple, w, *, tm_in=512, tm_merge=512, tm_ffn=256, ret_chunk=256, attn_blk=512):
    nb, seq, _ = x.shape
    n = nb * seq
    tm_in, tm_merge, tm_ffn = min(tm_in, seq), min(tm_merge, seq), min(tm_ffn, seq)
    ret_chunk, attn_blk = min(ret_chunk, seq), min(attn_blk, seq)
    x2d = x.reshape(n, D_MODEL)
    cos, sin = _rope_tables(jnp.arange(seq))
    rqk, rv, rg, dq, dk, dkb, dv, dvb, ga, gb = _in_proj(
        x2d, w["g_mix"], w["w_in"], cos, sin, w["g_qn"], w["g_kn"], tm=tm_in)

    n_chunks = seq // ret_chunk
    s0 = jnp.zeros((nb, RET_HEADS, RET_DK, RET_DV), f32)
    oa, s_new = _retention(rqk, rv, rg, s0, chunk=ret_chunk, n_chunks=n_chunks,
                           row_index=lambda b, c: (b * n_chunks + c, 0))
    ob = _attn_prompt(dq, dkb, dvb, w["lam"], w["g_dnorm"], nb=nb, seq=seq, blk=attn_blk)
    x1 = _merge(x2d, oa, ob, ga, gb, w["w_ret_br"], w["w_da_br"], w["w_out"], tm=tm_merge)

    cstate = jnp.zeros((nb, SUBLANES, 2 * D_FF), f32)
    y, ctail = _ffn(x1, ple.reshape(n, -1), cstate, w["g_ffn"], w["w_up"], w["conv_w"], w["conv_b"],
                    w["w_down"], w["g_ple"], w["w_ple_gate"], w["w_ple"], tm=tm_ffn, shift=1)
    return (y.reshape(nb, seq, D_MODEL),
            dk.reshape(nb, seq, DA_HEADS, 2, DA_DK), dv.reshape(nb, seq, DA_HEADS, DA_DV),
            s_new, ctail[:, SUBLANES - (CONV_W - 1):])


def _layer_decode(x, ple, cache_k, cache_v, state_ret, state_conv, w, *, past_len):
    nb, t, _ = x.shape
    n = nb * t
    past = cache_k.shape[1]
    tm = n

    def to_tm(a):
        return jnp.swapaxes(a, 0, 1).reshape(n, a.shape[-1])

    def from_tm(a):
        return jnp.swapaxes(a.reshape(t, nb, a.shape[-1]), 0, 1)

    cos, sin = _rope_tables(past_len + jnp.arange(n) // nb)
    rqk, rv, rg, dq, dk, dkb, dv, dvb, ga, gb = _in_proj(
        to_tm(x), w["g_mix"], w["w_in"], cos, sin, w["g_qn"], w["g_kn"], tm=tm)

    view = lambda a: a.reshape(t, nb * IN_COLS)
    oa, s_new = _retention(view(rqk), view(rv), view(rg), state_ret, chunk=t, n_chunks=1,
                           row_index=lambda b, c: (0, b))
    ob = _attn_decode(view(dq), view(dkb), view(dvb),
                      cache_k.reshape(nb * past, DA_HEADS * 2 * DA_DK),
                      cache_v.reshape(nb * past, DA_HEADS * DA_DV),
                      w["lam"], w["g_dnorm"], nb=nb, t=t, past=past)
    x1 = _merge(to_tm(x), oa.reshape(n, IN_COLS), ob.reshape(n, IN_COLS), ga, gb,
                w["w_ret_br"], w["w_da_br"], w["w_out"], tm=tm)

    halo = (CONV_W - 1) * nb
    cstate = jnp.swapaxes(state_conv, 0, 1).reshape(1, halo, 2 * D_FF)
    y, ctail = _ffn(x1, to_tm(ple), cstate, w["g_ffn"], w["w_up"], w["conv_w"], w["conv_b"],
                    w["w_down"], w["g_ple"], w["w_ple_gate"], w["w_ple"], tm=tm, shift=nb)
    return (from_tm(y),
            from_tm(dk).reshape(nb, t, DA_HEADS, 2, DA_DK), from_tm(dv).reshape(nb, t, DA_HEADS, DA_DV),
            s_new, jnp.swapaxes(ctail.reshape(CONV_W - 1, nb, 2 * D_FF), 0, 1))


def _layer_weights(i, g_mix, w_in, g_qn, g_kn, lam_q1, lam_k1, lam_q2, lam_k2, g_dnorm, w_ret_br,
                   w_da_br, w_out, g_ffn, w_up, conv_w, conv_b, w_down, g_ple, w_ple_gate, w_ple):
    vec = lambda a: a[i].reshape(1, -1)
    return dict(
        g_mix=vec(g_mix), w_in=w_in[i].astype(bf16), g_qn=vec(g_qn), g_kn=vec(g_kn),
        lam=(vec(lam_q1), vec(lam_k1), vec(lam_q2), vec(lam_k2)), g_dnorm=vec(g_dnorm),
        w_ret_br=w_ret_br[i].astype(bf16), w_da_br=w_da_br[i].astype(bf16),
        w_out=w_out[i].astype(bf16), g_ffn=vec(g_ffn), w_up=w_up[i].astype(bf16),
        conv_w=conv_w[i], conv_b=vec(conv_b), w_down=w_down[i].astype(bf16), g_ple=vec(g_ple),
        w_ple_gate=w_ple_gate[i].astype(bf16), w_ple=w_ple[i].astype(bf16))


def kernel(x_prompt, x_sample, cache_k, cache_v, state_ret, state_conv, p_prompt, p_sample, g_mix, w_in, g_qn, g_kn, lam_q1, lam_k1, lam_q2, lam_k2, g_dnorm, w_ret_br, w_da_br, w_out, g_ffn, w_up, conv_w, conv_b, w_down, g_ple, w_ple_gate, w_ple):
    depth = w_in.shape[0]
    yp, ys = x_prompt, x_sample
    outs = [[] for _ in range(8)]
    for i in range(depth):
        w = _layer_weights(i, g_mix, w_in, g_qn, g_kn, lam_q1, lam_k1, lam_q2, lam_k2, g_dnorm,
                           w_ret_br, w_da_br, w_out, g_ffn, w_up, conv_w, conv_b, w_down, g_ple,
                           w_ple_gate, w_ple)
        yp, k1, v1, s1, c1 = _layer_prompt(yp, p_prompt[i], w)
        ys, k2, v2, s2, c2 = _layer_decode(ys, p_sample[i], cache_k[i], cache_v[i], state_ret[i],
                                           state_conv[i], w, past_len=PAST_LEN)
        for lst, val in zip(outs, (k1, k2, v1, v2, s1, s2, c1, c2)):
            lst.append(val)
    return (yp, ys) + tuple(jnp.stack(o) for o in outs)
```

```python
import functools
import math

import jax
import jax.numpy as jnp
import numpy as np
from jax import lax
from jax.experimental import pallas as pl
from jax.experimental.pallas import tpu as pltpu

D_MODEL = 1024
CHUNK = 64
RET_HEADS = 4
RET_DK = 128
RET_DV = 256
DA_HEADS = 4
DA_DK = 128
DA_DV = 256
D_FF = 2816
CONV_W = 3
ROPE_THETA = 10000.0
EPS = 1e-6
PAST_LEN = 1024
LAM_INIT = 0.8 - 0.6 * math.exp(-0.3 * 0)
LOG2E = math.log2(math.e)

LANES = 128
SUBLANES = 8
IN_COLS = 1024
N_IN_BLOCKS = 8
VMEM_LIMIT = 56 * 1024 * 1024

f32 = jnp.float32
bf16 = jnp.bfloat16

NT_DIMS = (((1,), (1,)), ((), ()))
TN_DIMS = (((0,), (0,)), ((), ()))


def _rms(x):
    return x * lax.rsqrt(jnp.mean(x * x, axis=-1, keepdims=True) + EPS)


def _rope(x, cos, sin_signed):
    return x * cos + pltpu.roll(x, RET_DK // 2, 1) * sin_signed


def _sigmoid(x):
    return 0.5 * jnp.tanh(0.5 * x) + 0.5


def _params(sem):
    return pltpu.CompilerParams(dimension_semantics=sem, vmem_limit_bytes=VMEM_LIMIT)


def _const_spec(shape):
    nd = len(shape)
    return pl.BlockSpec(shape, lambda *_: (0,) * nd)


def _in_proj_kernel(x_ref, gmix_ref, w_ref, cos_ref, sin_ref, gq_ref, gk_ref,
                    rqk_ref, rv_ref, rg_ref, dq_ref, dk_ref, dkb_ref, dv_ref, dvb_ref,
                    ga_ref, gb_ref):
    h = (_rms(x_ref[...]) * gmix_ref[...]).astype(bf16)
    cos = cos_ref[...]
    sin = sin_ref[...]
    n_heads = IN_COLS // LANES

    def block(j):
        return jnp.dot(h, w_ref[:, j * IN_COLS:(j + 1) * IN_COLS], preferred_element_type=f32)

    def heads(z):
        return [(slice(g * LANES, (g + 1) * LANES), z[:, g * LANES:(g + 1) * LANES])
                for g in range(n_heads)]

    for g, (sl, zg) in enumerate(heads(block(0))):
        r = _rope(zg, cos, sin)
        if g >= RET_HEADS:
            r = r * (RET_DK ** -0.5)
        rqk_ref[:, sl] = r.astype(bf16)

    rv_ref[...] = block(1).astype(bf16)
    z = block(2)
    rg_ref[...] = (z * _sigmoid(z)).astype(bf16)

    for sl, zg in heads(block(3)):
        r = _rope(_rms(zg) * gq_ref[...], cos, sin) * (DA_DK ** -0.5 * LOG2E)
        dq_ref[:, sl] = r.astype(bf16)

    tm = x_ref.shape[0]
    for g, (sl, zg) in enumerate(heads(block(4))):
        r = _rope(_rms(zg) * gk_ref[...], cos, sin)
        dk_ref[pl.ds(g, tm, stride=n_heads), :] = r
        dkb_ref[:, sl] = r.astype(bf16)

    z = block(5)
    for g, (sl, zg) in enumerate(heads(z)):
        head, half = divmod(g, 2)
        dv_ref[pl.ds(half * DA_HEADS + head, tm, stride=n_heads), :] = zg
    dvb_ref[...] = z.astype(bf16)
    ga_ref[...] = _sigmoid(block(6)).astype(bf16)
    gb_ref[...] = _sigmoid(block(7)).astype(bf16)


def _resident(shape):
    nd = len(shape)
    return pl.BlockSpec(shape, lambda *_: (0,) * nd, pipeline_mode=pl.Buffered(1))


def _in_proj(x2d, g_mix, w_in, cos, sin, g_qn, g_kn, *, tm):
    n = x2d.shape[0]
    pos_blocks = cos.shape[0] // tm
    row = pl.BlockSpec((tm, IN_COLS), lambda i: (i, 0))
    tab = pl.BlockSpec((tm, LANES), lambda i: (i % pos_blocks, 0))
    bf = jax.ShapeDtypeStruct((n, IN_COLS), bf16)
    chunks = IN_COLS // LANES
    fl = jax.ShapeDtypeStruct((n * chunks, LANES), f32)
    lin = pl.BlockSpec((tm * chunks, LANES), lambda i: (i, 0))
    return pl.pallas_call(
        _in_proj_kernel,
        grid=(n // tm,),
        in_specs=[row, _resident((1, D_MODEL)), _resident((D_MODEL, N_IN_BLOCKS * IN_COLS)),
                  tab, tab, _resident((1, LANES)), _resident((1, LANES))],
        out_specs=[row, row, row, row, lin, row, lin, row, row, row],
        out_shape=[bf, bf, bf, bf, fl, bf, fl, bf, bf, bf],
        compiler_params=_params(("parallel",)),
        name="in_proj",
    )(x2d, g_mix, w_in, cos, sin, g_qn, g_kn)


def _ret_log_gamma():
    return np.log(1.0 - np.exp2(-5.0 - np.arange(RET_HEADS, dtype=np.float64)))


def _ret_tables(c):
    log_g = jnp.asarray(_ret_log_gamma(), f32)[:, None, None]
    idx = jnp.arange(c, dtype=f32)
    diff = idx[:, None] - idx[None, :]
    decay = jnp.where(diff >= 0, jnp.exp(log_g * jnp.maximum(diff, 0.0)), 0.0)
    qd = jnp.broadcast_to(jnp.exp(log_g * (idx + 1.0)[None, :, None]), (RET_HEADS, c, RET_DV))
    kd = jnp.broadcast_to(jnp.exp(log_g * (c - 1.0 - idx)[None, :, None]), (RET_HEADS, c, RET_DK))
    return decay, qd, kd


def _retention_kernel(qk_ref, v_ref, g_ref, dec_ref, qd_ref, kd_ref, s0_ref, o_ref, s_ref, *, chunk):
    @pl.when(pl.program_id(1) == 0)
    def _():
        s_ref[...] = s0_ref[...]

    log_g = _ret_log_gamma()
    for h in range(RET_HEADS):
        q = qk_ref[:, h * RET_DK:(h + 1) * RET_DK]
        k = qk_ref[:, (RET_HEADS + h) * RET_DK:(RET_HEADS + h + 1) * RET_DK]
        v = v_ref[:, h * RET_DV:(h + 1) * RET_DV]
        state = s_ref[0, h]
        scores = lax.dot_general(q, k, NT_DIMS, preferred_element_type=f32) * dec_ref[h]
        intra = jnp.dot(scores.astype(bf16), v, preferred_element_type=f32)
        inter = jnp.dot(q, state.astype(bf16), preferred_element_type=f32) * qd_ref[h]
        kw = (k.astype(f32) * kd_ref[h]).astype(bf16)
        s_ref[0, h] = float(np.exp(chunk * log_g[h])) * state + lax.dot_general(
            kw, v, TN_DIMS, preferred_element_type=f32)
        o = _rms(intra + inter)
        gate = g_ref[:, h * RET_DV:(h + 1) * RET_DV].astype(f32)
        o_ref[:, h * RET_DV:(h + 1) * RET_DV] = (o * gate).astype(bf16)


def _retention(rqk, rv, rg, s0, *, chunk, n_chunks, row_index):
    nb = s0.shape[0]
    decay, qd, kd = _ret_tables(chunk)
    blk = pl.BlockSpec((chunk, IN_COLS), row_index)
    st = pl.BlockSpec((1, RET_HEADS, RET_DK, RET_DV), lambda b, c: (b, 0, 0, 0))
    return pl.pallas_call(
        functools.partial(_retention_kernel, chunk=chunk),
        grid=(nb, n_chunks),
        in_specs=[blk, blk, blk, _const_spec(decay.shape), _const_spec(qd.shape),
                  _const_spec(kd.shape), st],
        out_specs=[blk, st],
        out_shape=[jax.ShapeDtypeStruct(rqk.shape, bf16), jax.ShapeDtypeStruct(s0.shape, f32)],
        compiler_params=_params(("parallel", "arbitrary")),
        name="retention",
    )(rqk, rv, rg, decay, qd, kd, s0)


def _lam(lq1_ref, lk1_ref, lq2_ref, lk2_ref):
    e1 = jnp.exp(jnp.sum(lq1_ref[...] * lk1_ref[...], axis=-1, keepdims=True))
    e2 = jnp.exp(jnp.sum(lq2_ref[...] * lk2_ref[...], axis=-1, keepdims=True))
    return e1 - e2 + LAM_INIT


ATTN_SUB = 64


def _lane_tile(x, width):
    return jnp.concatenate([x] * (width // LANES), axis=1)


def _attn_prompt_kernel(q_ref, k_ref, v_ref, lq1_ref, lk1_ref, lq2_ref, lk2_ref, gd_ref,
                        o_ref, m_ref, l_ref, acc_ref, sa_ref, sb_ref, alpha_ref, p_ref, *, tk):
    tq = 2 * tk
    qi = pl.program_id(2)
    m_ref[...] = jnp.full(m_ref.shape, -1e30, f32)
    l_ref[...] = jnp.zeros(l_ref.shape, f32)
    acc_ref[...] = jnp.zeros(acc_ref.shape, f32)

    def scores(kb, s_ref, r0=0):
        start = pl.multiple_of(kb * tk, tk)
        for a in range(2):
            q = q_ref[r0:, a * DA_DK:(a + 1) * DA_DK]
            k = k_ref[pl.ds(start, tk), a * DA_DK:(a + 1) * DA_DK]
            s_ref[a, r0:, :] = lax.dot_general(q, k, NT_DIMS, preferred_element_type=f32)

    def softmax(slot, diag=None, r0=0):
        s_ref = (sa_ref, sb_ref)[slot]
        for a in range(2):
            buf = 2 * slot + a
            for r in range(r0 // ATTN_SUB, tq // ATTN_SUB):
                rows = slice(r * ATTN_SUB, (r + 1) * ATTN_SUB)
                s = s_ref[a, rows, :]
                if diag is not None and (diag * tk + tk - 1) // CHUNK > (r * ATTN_SUB) // CHUNK:
                    row = (lax.broadcasted_iota(jnp.int32, s.shape, 0) + r * ATTN_SUB) // CHUNK
                    col = (lax.broadcasted_iota(jnp.int32, s.shape, 1) + diag * tk) // CHUNK
                    s = jnp.where(col <= row, s, -1e30)
                m_prev = m_ref[a, rows, :]
                m_new = jnp.maximum(m_prev, jnp.max(s, axis=1, keepdims=True))
                alpha = jnp.exp2(m_prev - m_new)
                p = jnp.exp2(s - _lane_tile(m_new, tk))
                l_ref[a, rows, :] = alpha * l_ref[a, rows, :] + jnp.sum(p, axis=1, keepdims=True)
                m_ref[a, rows, :] = m_new
                alpha_ref[buf, rows, :] = alpha
                p_ref[buf, rows, :] = p.astype(bf16)

    def pv(kb, slot, r0=0):
        start = pl.multiple_of(kb * tk, tk)
        v = v_ref[pl.ds(start, tk), :]
        for a in range(2):
            buf = 2 * slot + a
            acc_ref[a, r0:, :] = acc_ref[a, r0:, :] * _lane_tile(alpha_ref[buf, r0:, :], DA_DV) + jnp.dot(
                p_ref[buf, r0:, :], v, preferred_element_type=f32)

    alpha_ref[2:4] = jnp.ones((2,) + alpha_ref.shape[1:], f32)
    p_ref[2:4] = jnp.zeros((2,) + p_ref.shape[1:], bf16)
    scores(0, sa_ref)

    def body(i, carry):
        kb = 2 * i
        pv(jnp.maximum(kb - 1, 0), 1)
        scores(kb + 1, sb_ref)
        softmax(0)
        pv(kb, 0)
        scores(kb + 2, sa_ref)
        softmax(1)
        return carry

    lax.fori_loop(0, qi, body, 0)
    pv(jnp.maximum(2 * qi - 1, 0), 1)
    scores(2 * qi + 1, sb_ref, r0=tk)
    softmax(0, diag=0)
    pv(2 * qi, 0)
    softmax(1, diag=1, r0=tk)
    pv(2 * qi + 1, 1, r0=tk)

    lam = _lam(lq1_ref, lk1_ref, lq2_ref, lk2_ref)
    o0 = acc_ref[0] * _lane_tile(1.0 / l_ref[0], DA_DV)
    o1 = acc_ref[1] * _lane_tile(1.0 / l_ref[1], DA_DV)
    o = _rms(o0 - lam * o1) * gd_ref[...] * (1.0 - LAM_INIT)
    o_ref[...] = o.astype(bf16)


def _attn_prompt(dq, dkb, dvb, lam_params, g_dnorm, *, nb, seq, tk):
    hw = 2 * DA_DK
    tq = 2 * tk
    qspec = pl.BlockSpec((tq, hw), lambda b, h, i: (b * (seq // tq) + i, h))
    kvspec = pl.BlockSpec((seq, hw), lambda b, h, i: (b, h))
    vec = _const_spec((1, DA_DK))
    return pl.pallas_call(
        functools.partial(_attn_prompt_kernel, tk=tk),
        grid=(nb, DA_HEADS, seq // tq),
        in_specs=[qspec, kvspec, kvspec, vec, vec, vec, vec, _const_spec((1, DA_DV))],
        out_specs=qspec,
        out_shape=jax.ShapeDtypeStruct(dq.shape, bf16),
        scratch_shapes=[pltpu.VMEM((2, tq, LANES), f32), pltpu.VMEM((2, tq, LANES), f32),
                        pltpu.VMEM((2, tq, DA_DV), f32), pltpu.VMEM((2, tq, tk), f32),
                        pltpu.VMEM((2, tq, tk), f32), pltpu.VMEM((4, tq, LANES), f32),
                        pltpu.VMEM((4, tq, tk), bf16)],
        compiler_params=_params(("parallel", "parallel", "arbitrary")),
        name="attn_prompt",
    )(dq, dkb, dvb, *lam_params, g_dnorm)


def _attn_decode_kernel(q_ref, kc_ref, vc_ref, kn_ref, vn_ref, lq1_ref, lk1_ref, lq2_ref, lk2_ref,
                        gd_ref, o_ref, *, past):
    lam = _lam(lq1_ref, lk1_ref, lq2_ref, lk2_ref)
    rows_per_token = DA_HEADS * 2
    for h in range(DA_HEADS):
        hs = slice(h * DA_DV, (h + 1) * DA_DV)
        probs = []
        for a in range(2):
            cols = slice(h * DA_DV + a * DA_DK, h * DA_DV + (a + 1) * DA_DK)
            q = q_ref[:, cols]
            kc = kc_ref[pl.ds(h * 2 + a, past, stride=rows_per_token), :].astype(bf16)
            kn = kn_ref[:, cols]
            sc = lax.dot_general(q, kc, NT_DIMS, preferred_element_type=f32)
            sn = lax.dot_general(q, kn, NT_DIMS, preferred_element_type=f32)
            m = jnp.maximum(jnp.max(sc, axis=1, keepdims=True), jnp.max(sn, axis=1, keepdims=True))
            pc = jnp.exp2(sc - m)
            pn = jnp.exp2(sn - m)
            inv = 1.0 / (jnp.sum(pc, axis=1, keepdims=True) + jnp.sum(pn, axis=1, keepdims=True))
            probs.append((pc * inv, pn * inv))
        ac = probs[0][0] - lam * probs[1][0]
        an = probs[0][1] - lam * probs[1][1]
        o = jnp.dot(ac.astype(bf16), vc_ref[:, hs].astype(bf16), preferred_element_type=f32)
        o = o + jnp.dot(an.astype(bf16), vn_ref[:, hs], preferred_element_type=f32)
        o_ref[:, hs] = (_rms(o) * gd_ref[...] * (1.0 - LAM_INIT)).astype(bf16)


def _attn_decode(dq_v, dkb_v, dvb_v, cache_k_rows, cache_v2d, lam_params, g_dnorm, *, nb, t, past):
    new = pl.BlockSpec((t, IN_COLS), lambda b: (0, b))
    old_k = pl.BlockSpec((past * DA_HEADS * 2, DA_DK), lambda b: (b, 0))
    old_v = pl.BlockSpec((past, IN_COLS), lambda b: (b, 0))
    vec = _const_spec((1, DA_DK))
    return pl.pallas_call(
        functools.partial(_attn_decode_kernel, past=past),
        grid=(nb,),
        in_specs=[new, old_k, old_v, new, new, vec, vec, vec, vec, _const_spec((1, DA_DV))],
        out_specs=new,
        out_shape=jax.ShapeDtypeStruct(dq_v.shape, bf16),
        compiler_params=_params(("parallel",)),
        name="attn_decode",
    )(dq_v, cache_k_rows, cache_v2d, dkb_v, dvb_v, *lam_params, g_dnorm)


def _merge_kernel(x_ref, oa_ref, ob_ref, ga_ref, gb_ref, wr_ref, wd_ref, wo_ref, y_ref):
    a = jnp.dot(oa_ref[...], wr_ref[...], preferred_element_type=f32)
    b = jnp.dot(ob_ref[...], wd_ref[...], preferred_element_type=f32)
    m = ga_ref[...].astype(f32) * a + gb_ref[...].astype(f32) * b
    y_ref[...] = x_ref[...] + jnp.dot(m.astype(bf16), wo_ref[...], preferred_element_type=f32)


def _merge(x2d, oa, ob, ga, gb, w_ret_br, w_da_br, w_out, *, tm):
    n = x2d.shape[0]
    row = pl.BlockSpec((tm, D_MODEL), lambda i: (i, 0))
    w = _const_spec((D_MODEL, D_MODEL))
    return pl.pallas_call(
        _merge_kernel,
        grid=(n // tm,),
        in_specs=[row] * 5 + [w] * 3,
        out_specs=row,
        out_shape=jax.ShapeDtypeStruct(x2d.shape, f32),
        compiler_params=_params(("parallel",)),
        name="merge",
    )(x2d, oa, ob, ga, gb, w_ret_br, w_da_br, w_out)


FF_CHUNK = 256


def _ffn_kernel(x_ref, ple_ref, cst_ref, gffn_ref, wup_ref, cw_ref, cb_ref, wdn_ref, gple_ref,
                wpg_ref, wple_ref, y_ref, cnew_ref, uext_ref, act_ref, *, tm, shift, halo):
    @pl.when(pl.program_id(1) == 0)
    def _():
        uext_ref[0:halo, :] = cst_ref[0]

    x1 = x_ref[...]
    xn = (_rms(x1) * gffn_ref[...]).astype(bf16)
    uext_ref[halo:halo + tm, :] = jnp.dot(xn, wup_ref[...], preferred_element_type=f32)

    def conv(col):
        sl = slice(col, col + FF_CHUNK)
        acc = cw_ref[0:1, sl] * uext_ref[halo - 2 * shift:halo - 2 * shift + tm, sl]
        acc = acc + cw_ref[1:2, sl] * uext_ref[halo - shift:halo - shift + tm, sl]
        acc = acc + cw_ref[2:3, sl] * uext_ref[halo:halo + tm, sl]
        return acc + cb_ref[:, sl]

    for c in range(D_FF // FF_CHUNK):
        cg = conv(c * FF_CHUNK)
        cv = conv(D_FF + c * FF_CHUNK)
        act_ref[:, c * FF_CHUNK:(c + 1) * FF_CHUNK] = (cg * _sigmoid(cg) * cv).astype(bf16)

    x2 = x1 + jnp.dot(act_ref[...], wdn_ref[...], preferred_element_type=f32)
    tail = uext_ref[tm:tm + halo, :]
    cnew_ref[0] = tail
    uext_ref[0:halo, :] = tail

    gate = jnp.dot((_rms(x2) * gple_ref[...]).astype(bf16), wpg_ref[...], preferred_element_type=f32)
    emb = jnp.dot(ple_ref[...].astype(bf16), wple_ref[...], preferred_element_type=f32)
    y_ref[...] = x2 + _sigmoid(gate) * emb


def _ffn(x1, ple2d, cstate, g_ffn, w_up, conv_w, conv_b, w_down, g_ple, w_ple_gate, w_ple, *,
         tm, shift):
    n = x1.shape[0]
    nseg, halo, _ = cstate.shape
    tiles = n // (nseg * tm)
    ple_dim = ple2d.shape[1]
    resident = _resident
    row = lambda width: pl.BlockSpec((tm, width), lambda s, t: (s * tiles + t, 0))
    seg = pl.BlockSpec((1, halo, 2 * D_FF), lambda s, t: (s, 0, 0))
    return pl.pallas_call(
        functools.partial(_ffn_kernel, tm=tm, shift=shift, halo=halo),
        grid=(nseg, tiles),
        in_specs=[row(D_MODEL), row(ple_dim), seg, resident((1, D_MODEL)),
                  resident((D_MODEL, 2 * D_FF)), resident((CONV_W, 2 * D_FF)),
                  resident((1, 2 * D_FF)), resident((D_FF, D_MODEL)), resident((1, D_MODEL)),
                  resident((D_MODEL, D_MODEL)), resident((ple_dim, D_MODEL))],
        out_specs=[row(D_MODEL), seg],
        out_shape=[jax.ShapeDtypeStruct(x1.shape, f32), jax.ShapeDtypeStruct(cstate.shape, f32)],
        scratch_shapes=[pltpu.VMEM((halo + tm, 2 * D_FF), f32), pltpu.VMEM((tm, D_FF), bf16)],
        compiler_params=_params(("parallel", "arbitrary")),
        name="ffn",
    )(x1, ple2d, cstate, g_ffn, w_up, conv_w, conv_b, w_down, g_ple, w_ple_gate, w_ple)


def _rope_tables(pos):
    inv = ROPE_THETA ** (-jnp.arange(0, RET_DK, 2, dtype=f32) / RET_DK)
    ang = pos.astype(f32)[:, None] * inv[None, :]
    ang = jnp.concatenate([ang, ang], axis=-1)
    sign = jnp.where(jnp.arange(RET_DK) < RET_DK // 2, -1.0, 1.0).astype(f32)
    return jnp.cos(ang), jnp.sin(ang) * sign


def _layer_prompt(x, ple, w, *, tm_in=512, tm_merge=512, tm_ffn=512, ret_chunk=256, attn_tk=512):
    nb, seq, _ = x.shape
    n = nb * seq
    tm_in, tm_merge, tm_ffn = min(tm_in, seq), min(tm_merge, seq), min(tm_ffn, seq)
    ret_chunk, attn_tk = min(ret_chunk, seq), min(attn_tk, seq // 2)
    x2d = x.reshape(n, D_MODEL)
    cos, sin = _rope_tables(jnp.arange(seq))
    rqk, rv, rg, dq, dk, dkb, dv, dvb, ga, gb = _in_proj(
        x2d, w["g_mix"], w["w_in"], cos, sin, w["g_qn"], w["g_kn"], tm=tm_in)

    n_chunks = seq // ret_chunk
    s0 = jnp.zeros((nb, RET_HEADS, RET_DK, RET_DV), f32)
    oa, s_new = _retention(rqk, rv, rg, s0, chunk=ret_chunk, n_chunks=n_chunks,
                           row_index=lambda b, c: (b * n_chunks + c, 0))
    ob = _attn_prompt(dq, dkb, dvb, w["lam"], w["g_dnorm"], nb=nb, seq=seq, tk=attn_tk)
    x1 = _merge(x2d, oa, ob, ga, gb, w["w_ret_br"], w["w_da_br"], w["w_out"], tm=tm_merge)

    cstate = jnp.zeros((nb, SUBLANES, 2 * D_FF), f32)
    y, ctail = _ffn(x1, ple.reshape(n, -1), cstate, w["g_ffn"], w["w_up"], w["conv_w"], w["conv_b"],
                    w["w_down"], w["g_ple"], w["w_ple_gate"], w["w_ple"], tm=tm_ffn, shift=1)
    return (y.reshape(nb, seq, D_MODEL),
            dk.reshape(nb, seq, DA_HEADS, 2, DA_DK),
            jnp.swapaxes(dv.reshape(nb, seq, 2, DA_HEADS, LANES), 2, 3).reshape(nb, seq, DA_HEADS, DA_DV),
            s_new, ctail[:, SUBLANES - (CONV_W - 1):])


def _layer_decode(x, ple, cache_k, cache_v, state_ret, state_conv, w, *, past_len):
    nb, t, _ = x.shape
    n = nb * t
    past = cache_k.shape[1]
    tm = n

    def to_tm(a):
        return jnp.swapaxes(a, 0, 1).reshape(n, a.shape[-1])

    def from_tm(a):
        return jnp.swapaxes(a.reshape(t, nb, a.shape[-1]), 0, 1)

    cos, sin = _rope_tables(past_len + jnp.arange(n) // nb)
    rqk, rv, rg, dq, dk, dkb, dv, dvb, ga, gb = _in_proj(
        to_tm(x), w["g_mix"], w["w_in"], cos, sin, w["g_qn"], w["g_kn"], tm=tm)

    view = lambda a: a.reshape(t, nb * IN_COLS)
    oa, s_new = _retention(view(rqk), view(rv), view(rg), state_ret, chunk=t, n_chunks=1,
                           row_index=lambda b, c: (0, b))
    ob = _attn_decode(view(dq), view(dkb), view(dvb),
                      cache_k.reshape(nb * past * DA_HEADS * 2, DA_DK),
                      cache_v.reshape(nb * past, DA_HEADS * DA_DV),
                      w["lam"], w["g_dnorm"], nb=nb, t=t, past=past)
    x1 = _merge(to_tm(x), oa.reshape(n, IN_COLS), ob.reshape(n, IN_COLS), ga, gb,
                w["w_ret_br"], w["w_da_br"], w["w_out"], tm=tm)

    halo = (CONV_W - 1) * nb
    cstate = jnp.swapaxes(state_conv, 0, 1).reshape(1, halo, 2 * D_FF)
    y, ctail = _ffn(x1, to_tm(ple), cstate, w["g_ffn"], w["w_up"], w["conv_w"], w["conv_b"],
                    w["w_down"], w["g_ple"], w["w_ple_gate"], w["w_ple"], tm=tm, shift=nb)
    return (from_tm(y),
            jnp.swapaxes(dk.reshape(t, nb, DA_HEADS, 2, DA_DK), 0, 1),
            jnp.transpose(dv.reshape(t, nb, 2, DA_HEADS, LANES), (1, 0, 3, 2, 4)).reshape(
                nb, t, DA_HEADS, DA_DV),
            s_new, jnp.swapaxes(ctail.reshape(CONV_W - 1, nb, 2 * D_FF), 0, 1))


def _layer_weights(i, g_mix, w_in, g_qn, g_kn, lam_q1, lam_k1, lam_q2, lam_k2, g_dnorm, w_ret_br,
                   w_da_br, w_out, g_ffn, w_up, conv_w, conv_b, w_down, g_ple, w_ple_gate, w_ple):
    vec = lambda a: a[i].reshape(1, -1)
    return dict(
        g_mix=vec(g_mix), w_in=w_in[i].astype(bf16), g_qn=vec(g_qn), g_kn=vec(g_kn),
        lam=(vec(lam_q1), vec(lam_k1), vec(lam_q2), vec(lam_k2)), g_dnorm=vec(g_dnorm),
        w_ret_br=w_ret_br[i].astype(bf16), w_da_br=w_da_br[i].astype(bf16),
        w_out=w_out[i].astype(bf16), g_ffn=vec(g_ffn), w_up=w_up[i].astype(bf16),
        conv_w=conv_w[i], conv_b=vec(conv_b), w_down=w_down[i].astype(bf16), g_ple=vec(g_ple),
        w_ple_gate=w_ple_gate[i].astype(bf16), w_ple=w_ple[i].astype(bf16))


def kernel(x_prompt, x_sample, cache_k, cache_v, state_ret, state_conv, p_prompt, p_sample, g_mix, w_in, g_qn, g_kn, lam_q1, lam_k1, lam_q2, lam_k2, g_dnorm, w_ret_br, w_da_br, w_out, g_ffn, w_up, conv_w, conv_b, w_down, g_ple, w_ple_gate, w_ple):
    depth = w_in.shape[0]
    yp, ys = x_prompt, x_sample
    outs = [[] for _ in range(8)]
    for i in range(depth):
        w = _layer_weights(i, g_mix, w_in, g_qn, g_kn, lam_q1, lam_k1, lam_q2, lam_k2, g_dnorm,
                           w_ret_br, w_da_br, w_out, g_ffn, w_up, conv_w, conv_b, w_down, g_ple,
                           w_ple_gate, w_ple)
        yp, k1, v1, s1, c1 = _layer_prompt(yp, p_prompt[i], w)
        ys, k2, v2, s2, c2 = _layer_decode(ys, p_sample[i], cache_k[i], cache_v[i], state_ret[i],
                                           state_conv[i], w, past_len=PAST_LEN)
        for lst, val in zip(outs, (k1, k2, v1, v2, s1, s2, c1, c2)):
            lst.append(val)
    return (yp, ys) + tuple(jnp.stack(o) for o in outs)
```

```python
import functools
import math

import jax
import jax.numpy as jnp
import numpy as np
from jax import lax
from jax.experimental import pallas as pl
from jax.experimental.pallas import tpu as pltpu

D_MODEL = 1024
CHUNK = 64
RET_HEADS = 4
RET_DK = 128
RET_DV = 256
DA_HEADS = 4
DA_DK = 128
DA_DV = 256
D_FF = 2816
CONV_W = 3
ROPE_THETA = 10000.0
EPS = 1e-6
PAST_LEN = 1024
LAM_INIT = 0.8 - 0.6 * math.exp(-0.3 * 0)
LOG2E = math.log2(math.e)

LANES = 128
SUBLANES = 8
IN_COLS = 1024
N_IN_BLOCKS = 8
VMEM_LIMIT = 56 * 1024 * 1024

f32 = jnp.float32
bf16 = jnp.bfloat16

NT_DIMS = (((1,), (1,)), ((), ()))
TN_DIMS = (((0,), (0,)), ((), ()))


def _rms(x):
    return x * lax.rsqrt(jnp.mean(x * x, axis=-1, keepdims=True) + EPS)


def _rope(x, cos, sin_signed):
    return x * cos + pltpu.roll(x, RET_DK // 2, 1) * sin_signed


def _sigmoid(x):
    return 0.5 * jnp.tanh(0.5 * x) + 0.5


def _params(sem):
    return pltpu.CompilerParams(dimension_semantics=sem, vmem_limit_bytes=VMEM_LIMIT)


def _const_spec(shape):
    nd = len(shape)
    return pl.BlockSpec(shape, lambda *_: (0,) * nd)


def _in_proj_kernel(x_ref, gmix_ref, w_ref, cos_ref, sin_ref, gq_ref, gk_ref,
                    rqk_ref, rv_ref, rg_ref, dq_ref, dk_ref, dkb_ref, dv_ref, dvb_ref,
                    ga_ref, gb_ref):
    h = (_rms(x_ref[...]) * gmix_ref[...]).astype(bf16)
    cos = cos_ref[...]
    sin = sin_ref[...]
    n_heads = IN_COLS // LANES

    def block(j):
        return jnp.dot(h, w_ref[:, j * IN_COLS:(j + 1) * IN_COLS], preferred_element_type=f32)

    def heads(z):
        return [(slice(g * LANES, (g + 1) * LANES), z[:, g * LANES:(g + 1) * LANES])
                for g in range(n_heads)]

    for g, (sl, zg) in enumerate(heads(block(0))):
        r = _rope(zg, cos, sin)
        if g >= RET_HEADS:
            r = r * (RET_DK ** -0.5)
        rqk_ref[:, sl] = r.astype(bf16)

    rv_ref[...] = block(1).astype(bf16)
    z = block(2)
    rg_ref[...] = (z * _sigmoid(z)).astype(bf16)

    for sl, zg in heads(block(3)):
        r = _rope(_rms(zg) * gq_ref[...], cos, sin) * (DA_DK ** -0.5 * LOG2E)
        dq_ref[:, sl] = r.astype(bf16)

    tm = x_ref.shape[0]
    for g, (sl, zg) in enumerate(heads(block(4))):
        r = _rope(_rms(zg) * gk_ref[...], cos, sin)
        dk_ref[pl.ds(g, tm, stride=n_heads), :] = r
        dkb_ref[:, sl] = r.astype(bf16)

    z = block(5)
    for g, (sl, zg) in enumerate(heads(z)):
        head, half = divmod(g, 2)
        dv_ref[pl.ds(half * DA_HEADS + head, tm, stride=n_heads), :] = zg
    dvb_ref[...] = z.astype(bf16)
    ga_ref[...] = _sigmoid(block(6)).astype(bf16)
    gb_ref[...] = _sigmoid(block(7)).astype(bf16)


def _resident(shape):
    nd = len(shape)
    return pl.BlockSpec(shape, lambda *_: (0,) * nd, pipeline_mode=pl.Buffered(1))


def _in_proj(x2d, g_mix, w_in, cos, sin, g_qn, g_kn, *, tm):
    n = x2d.shape[0]
    pos_blocks = cos.shape[0] // tm
    row = pl.BlockSpec((tm, IN_COLS), lambda i: (i, 0))
    tab = pl.BlockSpec((tm, LANES), lambda i: (i % pos_blocks, 0))
    bf = jax.ShapeDtypeStruct((n, IN_COLS), bf16)
    chunks = IN_COLS // LANES
    fl = jax.ShapeDtypeStruct((n * chunks, LANES), f32)
    lin = pl.BlockSpec((tm * chunks, LANES), lambda i: (i, 0))
    return pl.pallas_call(
        _in_proj_kernel,
        grid=(n // tm,),
        in_specs=[row, _resident((1, D_MODEL)), _resident((D_MODEL, N_IN_BLOCKS * IN_COLS)),
                  tab, tab, _resident((1, LANES)), _resident((1, LANES))],
        out_specs=[row, row, row, row, lin, row, lin, row, row, row],
        out_shape=[bf, bf, bf, bf, fl, bf, fl, bf, bf, bf],
        compiler_params=_params(("parallel",)),
        name="in_proj",
    )(x2d, g_mix, w_in, cos, sin, g_qn, g_kn)


def _ret_log_gamma():
    return np.log(1.0 - np.exp2(-5.0 - np.arange(RET_HEADS, dtype=np.float64)))


def _ret_tables(c):
    log_g = jnp.asarray(_ret_log_gamma(), f32)[:, None, None]
    idx = jnp.arange(c, dtype=f32)
    diff = idx[:, None] - idx[None, :]
    decay = jnp.where(diff >= 0, jnp.exp(log_g * jnp.maximum(diff, 0.0)), 0.0)
    qd = jnp.broadcast_to(jnp.exp(log_g * (idx + 1.0)[None, :, None]), (RET_HEADS, c, RET_DV))
    kd = jnp.broadcast_to(jnp.exp(log_g * (c - 1.0 - idx)[None, :, None]), (RET_HEADS, c, RET_DK))
    return decay, qd, kd


def _retention_kernel(qk_ref, v_ref, g_ref, dec_ref, qd_ref, kd_ref, s0_ref, o_ref, s_ref, *, chunk):
    @pl.when(pl.program_id(1) == 0)
    def _():
        s_ref[...] = s0_ref[...]

    log_g = _ret_log_gamma()
    for h in range(RET_HEADS):
        q = qk_ref[:, h * RET_DK:(h + 1) * RET_DK]
        k = qk_ref[:, (RET_HEADS + h) * RET_DK:(RET_HEADS + h + 1) * RET_DK]
        v = v_ref[:, h * RET_DV:(h + 1) * RET_DV]
        state = s_ref[0, h]
        scores = lax.dot_general(q, k, NT_DIMS, preferred_element_type=f32) * dec_ref[h]
        intra = jnp.dot(scores.astype(bf16), v, preferred_element_type=f32)
        inter = jnp.dot(q, state.astype(bf16), preferred_element_type=f32) * qd_ref[h]
        kw = (k.astype(f32) * kd_ref[h]).astype(bf16)
        s_ref[0, h] = float(np.exp(chunk * log_g[h])) * state + lax.dot_general(
            kw, v, TN_DIMS, preferred_element_type=f32)
        o = _rms(intra + inter)
        gate = g_ref[:, h * RET_DV:(h + 1) * RET_DV].astype(f32)
        o_ref[:, h * RET_DV:(h + 1) * RET_DV] = (o * gate).astype(bf16)


def _retention(rqk, rv, rg, s0, *, chunk, n_chunks, row_index):
    nb = s0.shape[0]
    decay, qd, kd = _ret_tables(chunk)
    blk = pl.BlockSpec((chunk, IN_COLS), row_index)
    st = pl.BlockSpec((1, RET_HEADS, RET_DK, RET_DV), lambda b, c: (b, 0, 0, 0))
    return pl.pallas_call(
        functools.partial(_retention_kernel, chunk=chunk),
        grid=(nb, n_chunks),
        in_specs=[blk, blk, blk, _const_spec(decay.shape), _const_spec(qd.shape),
                  _const_spec(kd.shape), st],
        out_specs=[blk, st],
        out_shape=[jax.ShapeDtypeStruct(rqk.shape, bf16), jax.ShapeDtypeStruct(s0.shape, f32)],
        compiler_params=_params(("parallel", "arbitrary")),
        name="retention",
    )(rqk, rv, rg, decay, qd, kd, s0)


def _lam(lq1_ref, lk1_ref, lq2_ref, lk2_ref):
    e1 = jnp.exp(jnp.sum(lq1_ref[...] * lk1_ref[...], axis=-1, keepdims=True))
    e2 = jnp.exp(jnp.sum(lq2_ref[...] * lk2_ref[...], axis=-1, keepdims=True))
    return e1 - e2 + LAM_INIT


ATTN_SUB = 64
FAST_SCORE_MAX = 30.0
SCORE_BOUND_PER_GAIN = 1.02 * LOG2E * DA_DK ** 0.5


def _lane_tile(x, width):
    return jnp.concatenate([x] * (width // LANES), axis=1)


def _attn_prompt_kernel(q_ref, k_ref, v_ref, lq1_ref, lk1_ref, lq2_ref, lk2_ref, gd_ref, gq_ref, gk_ref,
                        dmask_ref, o_ref, m_ref, l_ref, acc_ref, sa_ref, sb_ref, alpha_ref, p_ref, *, tk):
    tq = 2 * tk
    qi = pl.program_id(2)
    l_ref[...] = jnp.zeros(l_ref.shape, f32)
    acc_ref[...] = jnp.zeros(acc_ref.shape, f32)

    gq_max = jnp.max(jnp.abs(gq_ref[...]), axis=1, keepdims=True)
    gk_max = jnp.max(jnp.abs(gk_ref[...]), axis=1, keepdims=True)
    fast = (gq_max * gk_max)[0, 0] * SCORE_BOUND_PER_GAIN <= FAST_SCORE_MAX

    def probs(kb, slot, diag=None, r0=0):
        start = pl.multiple_of(kb * tk, tk)
        for a in range(2):
            buf = 2 * slot + a
            q = q_ref[r0:, a * DA_DK:(a + 1) * DA_DK]
            k = k_ref[pl.ds(start, tk), a * DA_DK:(a + 1) * DA_DK]
            s = lax.dot_general(q, k, NT_DIMS, preferred_element_type=f32)
            if diag == 0:
                s = jnp.concatenate([s[:tk] + dmask_ref[...], s[tk:]], axis=0)
            elif diag == 1:
                s = s + dmask_ref[...]
            p = jnp.exp2(s)
            part = p[:, 0:LANES]
            for c in range(1, tk // LANES):
                part = part + p[:, c * LANES:(c + 1) * LANES]
            l_ref[a, r0:, :] = l_ref[a, r0:, :] + part
            p_ref[buf, r0:, :] = p.astype(bf16)

    def pv_add(kb, slot, r0=0):
        start = pl.multiple_of(kb * tk, tk)
        v = v_ref[pl.ds(start, tk), :]
        for a in range(2):
            acc_ref[a, r0:, :] = acc_ref[a, r0:, :] + jnp.dot(
                p_ref[2 * slot + a, r0:, :], v, preferred_element_type=f32)

    def fast_path():
        def pair(i, last_diag):
            pv_add(2 * i, 0)
            probs(2 * i + 1, 1)
            pv_add(2 * i + 1, 1)
            probs(2 * i + 2, 0, diag=last_diag)

        @pl.when(qi == 0)
        def _():
            probs(0, 0, diag=0)

        @pl.when(qi > 0)
        def _():
            probs(0, 0)

            def body(i, carry):
                pair(i, None)
                return carry

            lax.fori_loop(0, qi - 1, body, 0)
            pair(qi - 1, 0)

        pv_add(2 * qi, 0)
        probs(2 * qi + 1, 1, diag=1, r0=tk)
        pv_add(2 * qi + 1, 1, r0=tk)
        for a in range(2):
            l_ref[a] = jnp.broadcast_to(jnp.sum(l_ref[a], axis=1, keepdims=True), l_ref.shape[1:])


    def scores(kb, s_ref, r0=0):
        start = pl.multiple_of(kb * tk, tk)
        for a in range(2):
            q = q_ref[r0:, a * DA_DK:(a + 1) * DA_DK]
            k = k_ref[pl.ds(start, tk), a * DA_DK:(a + 1) * DA_DK]
            s_ref[a, r0:, :] = lax.dot_general(q, k, NT_DIMS, preferred_element_type=f32)

    def softmax(slot, diag=None, r0=0):
        s_ref = (sa_ref, sb_ref)[slot]
        for a in range(2):
            buf = 2 * slot + a
            for r in range(r0 // ATTN_SUB, tq // ATTN_SUB):
                rows = slice(r * ATTN_SUB, (r + 1) * ATTN_SUB)
                s = s_ref[a, rows, :]
                if diag is not None and (diag * tk + tk - 1) // CHUNK > (r * ATTN_SUB) // CHUNK:
                    row = (lax.broadcasted_iota(jnp.int32, s.shape, 0) + r * ATTN_SUB) // CHUNK
                    col = (lax.broadcasted_iota(jnp.int32, s.shape, 1) + diag * tk) // CHUNK
                    s = jnp.where(col <= row, s, -1e30)
                m_prev = m_ref[a, rows, :]
                m_new = jnp.maximum(m_prev, jnp.max(s, axis=1, keepdims=True))
                alpha = jnp.exp2(m_prev - m_new)
                p = jnp.exp2(s - _lane_tile(m_new, tk))
                l_ref[a, rows, :] = alpha * l_ref[a, rows, :] + jnp.sum(p, axis=1, keepdims=True)
                m_ref[a, rows, :] = m_new
                alpha_ref[buf, rows, :] = alpha
                p_ref[buf, rows, :] = p.astype(bf16)

    def pv(kb, slot, r0=0):
        start = pl.multiple_of(kb * tk, tk)
        v = v_ref[pl.ds(start, tk), :]
        for a in range(2):
            buf = 2 * slot + a
            acc_ref[a, r0:, :] = acc_ref[a, r0:, :] * _lane_tile(alpha_ref[buf, r0:, :], DA_DV) + jnp.dot(
                p_ref[buf, r0:, :], v, preferred_element_type=f32)

    def safe_path():
        m_ref[...] = jnp.full(m_ref.shape, -1e30, f32)
        alpha_ref[2:4] = jnp.ones((2,) + alpha_ref.shape[1:], f32)
        p_ref[2:4] = jnp.zeros((2,) + p_ref.shape[1:], bf16)
        scores(0, sa_ref)

        def body(i, carry):
            kb = 2 * i
            pv(jnp.maximum(kb - 1, 0), 1)
            scores(kb + 1, sb_ref)
            softmax(0)
            pv(kb, 0)
            scores(kb + 2, sa_ref)
            softmax(1)
            return carry

        lax.fori_loop(0, qi, body, 0)
        pv(jnp.maximum(2 * qi - 1, 0), 1)
        scores(2 * qi + 1, sb_ref, r0=tk)
        softmax(0, diag=0)
        pv(2 * qi, 0)
        softmax(1, diag=1, r0=tk)
        pv(2 * qi + 1, 1, r0=tk)

    pl.when(fast)(fast_path)
    pl.when(jnp.logical_not(fast))(safe_path)

    lam = _lam(lq1_ref, lk1_ref, lq2_ref, lk2_ref)
    o0 = acc_ref[0] * _lane_tile(1.0 / l_ref[0], DA_DV)
    o1 = acc_ref[1] * _lane_tile(1.0 / l_ref[1], DA_DV)
    o = _rms(o0 - lam * o1) * gd_ref[...] * (1.0 - LAM_INIT)
    o_ref[...] = o.astype(bf16)


def _attn_prompt(dq, dkb, dvb, lam_params, g_dnorm, g_qn, g_kn, *, nb, seq, tk):
    hw = 2 * DA_DK
    tq = 2 * tk
    qspec = pl.BlockSpec((tq, hw), lambda b, h, i: (b * (seq // tq) + i, h))
    kvspec = pl.BlockSpec((seq, hw), lambda b, h, i: (b, h))
    vec = _const_spec((1, DA_DK))
    chunk_of = jnp.arange(tk) // CHUNK
    dmask = jnp.where(chunk_of[None, :] <= chunk_of[:, None], 0.0, -1e30).astype(f32)
    return pl.pallas_call(
        functools.partial(_attn_prompt_kernel, tk=tk),
        grid=(nb, DA_HEADS, seq // tq),
        in_specs=[qspec, kvspec, kvspec, vec, vec, vec, vec, _const_spec((1, DA_DV)), vec, vec,
                  _const_spec((tk, tk))],
        out_specs=qspec,
        out_shape=jax.ShapeDtypeStruct(dq.shape, bf16),
        scratch_shapes=[pltpu.VMEM((2, tq, LANES), f32), pltpu.VMEM((2, tq, LANES), f32),
                        pltpu.VMEM((2, tq, DA_DV), f32), pltpu.VMEM((2, tq, tk), f32),
                        pltpu.VMEM((2, tq, tk), f32), pltpu.VMEM((4, tq, LANES), f32),
                        pltpu.VMEM((4, tq, tk), bf16)],
        compiler_params=_params(("parallel", "parallel", "arbitrary")),
        name="attn_prompt",
    )(dq, dkb, dvb, *lam_params, g_dnorm, g_qn, g_kn, dmask)


def _attn_decode_kernel(q_ref, kc_ref, vc_ref, kn_ref, vn_ref, lq1_ref, lk1_ref, lq2_ref, lk2_ref,
                        gd_ref, o_ref, *, past):
    lam = _lam(lq1_ref, lk1_ref, lq2_ref, lk2_ref)
    rows_per_token = DA_HEADS * 2
    for h in range(DA_HEADS):
        hs = slice(h * DA_DV, (h + 1) * DA_DV)
        probs = []
        for a in range(2):
            cols = slice(h * DA_DV + a * DA_DK, h * DA_DV + (a + 1) * DA_DK)
            q = q_ref[:, cols]
            kc = kc_ref[pl.ds(h * 2 + a, past, stride=rows_per_token), :].astype(bf16)
            kn = kn_ref[:, cols]
            sc = lax.dot_general(q, kc, NT_DIMS, preferred_element_type=f32)
            sn = lax.dot_general(q, kn, NT_DIMS, preferred_element_type=f32)
            m = jnp.maximum(jnp.max(sc, axis=1, keepdims=True), jnp.max(sn, axis=1, keepdims=True))
            pc = jnp.exp2(sc - m)
            pn = jnp.exp2(sn - m)
            inv = 1.0 / (jnp.sum(pc, axis=1, keepdims=True) + jnp.sum(pn, axis=1, keepdims=True))
            probs.append((pc * inv, pn * inv))
        ac = probs[0][0] - lam * probs[1][0]
        an = probs[0][1] - lam * probs[1][1]
        o = jnp.dot(ac.astype(bf16), vc_ref[:, hs].astype(bf16), preferred_element_type=f32)
        o = o + jnp.dot(an.astype(bf16), vn_ref[:, hs], preferred_element_type=f32)
        o_ref[:, hs] = (_rms(o) * gd_ref[...] * (1.0 - LAM_INIT)).astype(bf16)


def _attn_decode(dq_v, dkb_v, dvb_v, cache_k_rows, cache_v2d, lam_params, g_dnorm, *, nb, t, past):
    new = pl.BlockSpec((t, IN_COLS), lambda b: (0, b))
    old_k = pl.BlockSpec((past * DA_HEADS * 2, DA_DK), lambda b: (b, 0))
    old_v = pl.BlockSpec((past, IN_COLS), lambda b: (b, 0))
    vec = _const_spec((1, DA_DK))
    return pl.pallas_call(
        functools.partial(_attn_decode_kernel, past=past),
        grid=(nb,),
        in_specs=[new, old_k, old_v, new, new, vec, vec, vec, vec, _const_spec((1, DA_DV))],
        out_specs=new,
        out_shape=jax.ShapeDtypeStruct(dq_v.shape, bf16),
        compiler_params=_params(("parallel",)),
        name="attn_decode",
    )(dq_v, cache_k_rows, cache_v2d, dkb_v, dvb_v, *lam_params, g_dnorm)


def _merge_kernel(x_ref, oa_ref, ob_ref, ga_ref, gb_ref, wr_ref, wd_ref, wo_ref, y_ref):
    a = jnp.dot(oa_ref[...], wr_ref[...], preferred_element_type=f32)
    b = jnp.dot(ob_ref[...], wd_ref[...], preferred_element_type=f32)
    m = ga_ref[...].astype(f32) * a + gb_ref[...].astype(f32) * b
    y_ref[...] = x_ref[...] + jnp.dot(m.astype(bf16), wo_ref[...], preferred_element_type=f32)


def _merge(x2d, oa, ob, ga, gb, w_ret_br, w_da_br, w_out, *, tm):
    n = x2d.shape[0]
    row = pl.BlockSpec((tm, D_MODEL), lambda i: (i, 0))
    w = _const_spec((D_MODEL, D_MODEL))
    return pl.pallas_call(
        _merge_kernel,
        grid=(n // tm,),
        in_specs=[row] * 5 + [w] * 3,
        out_specs=row,
        out_shape=jax.ShapeDtypeStruct(x2d.shape, f32),
        compiler_params=_params(("parallel",)),
        name="merge",
    )(x2d, oa, ob, ga, gb, w_ret_br, w_da_br, w_out)


FF_CHUNK = 256


def _ffn_kernel(x_ref, ple_ref, cst_ref, gffn_ref, wup_ref, cw_ref, cb_ref, wdn_ref, gple_ref,
                wpg_ref, wple_ref, y_ref, cnew_ref, uext_ref, act_ref, *, tm, shift, halo):
    @pl.when(pl.program_id(1) == 0)
    def _():
        uext_ref[0:halo, :] = cst_ref[0]

    x1 = x_ref[...]
    xn = (_rms(x1) * gffn_ref[...]).astype(bf16)
    uext_ref[halo:halo + tm, :] = jnp.dot(xn, wup_ref[...], preferred_element_type=f32)

    def conv(col):
        sl = slice(col, col + FF_CHUNK)
        acc = cw_ref[0:1, sl] * uext_ref[halo - 2 * shift:halo - 2 * shift + tm, sl]
        acc = acc + cw_ref[1:2, sl] * uext_ref[halo - shift:halo - shift + tm, sl]
        acc = acc + cw_ref[2:3, sl] * uext_ref[halo:halo + tm, sl]
        return acc + cb_ref[:, sl]

    for c in range(D_FF // FF_CHUNK):
        cg = conv(c * FF_CHUNK)
        cv = conv(D_FF + c * FF_CHUNK)
        act_ref[:, c * FF_CHUNK:(c + 1) * FF_CHUNK] = (cg * _sigmoid(cg) * cv).astype(bf16)

    x2 = x1 + jnp.dot(act_ref[...], wdn_ref[...], preferred_element_type=f32)
    tail = uext_ref[tm:tm + halo, :]
    cnew_ref[0] = tail
    uext_ref[0:halo, :] = tail

    gate = jnp.dot((_rms(x2) * gple_ref[...]).astype(bf16), wpg_ref[...], preferred_element_type=f32)
    emb = jnp.dot(ple_ref[...].astype(bf16), wple_ref[...], preferred_element_type=f32)
    y_ref[...] = x2 + _sigmoid(gate) * emb


def _ffn(x1, ple2d, cstate, g_ffn, w_up, conv_w, conv_b, w_down, g_ple, w_ple_gate, w_ple, *,
         tm, shift):
    n = x1.shape[0]
    nseg, halo, _ = cstate.shape
    tiles = n // (nseg * tm)
    ple_dim = ple2d.shape[1]
    resident = _resident
    row = lambda width: pl.BlockSpec((tm, width), lambda s, t: (s * tiles + t, 0))
    seg = pl.BlockSpec((1, halo, 2 * D_FF), lambda s, t: (s, 0, 0))
    return pl.pallas_call(
        functools.partial(_ffn_kernel, tm=tm, shift=shift, halo=halo),
        grid=(nseg, tiles),
        in_specs=[row(D_MODEL), row(ple_dim), seg, resident((1, D_MODEL)),
                  resident((D_MODEL, 2 * D_FF)), resident((CONV_W, 2 * D_FF)),
                  resident((1, 2 * D_FF)), resident((D_FF, D_MODEL)), resident((1, D_MODEL)),
                  resident((D_MODEL, D_MODEL)), resident((ple_dim, D_MODEL))],
        out_specs=[row(D_MODEL), seg],
        out_shape=[jax.ShapeDtypeStruct(x1.shape, f32), jax.ShapeDtypeStruct(cstate.shape, f32)],
        scratch_shapes=[pltpu.VMEM((halo + tm, 2 * D_FF), f32), pltpu.VMEM((tm, D_FF), bf16)],
        compiler_params=_params(("parallel", "arbitrary")),
        name="ffn",
    )(x1, ple2d, cstate, g_ffn, w_up, conv_w, conv_b, w_down, g_ple, w_ple_gate, w_ple)


def _rope_tables(pos):
    inv = ROPE_THETA ** (-jnp.arange(0, RET_DK, 2, dtype=f32) / RET_DK)
    ang = pos.astype(f32)[:, None] * inv[None, :]
    ang = jnp.concatenate([ang, ang], axis=-1)
    sign = jnp.where(jnp.arange(RET_DK) < RET_DK // 2, -1.0, 1.0).astype(f32)
    return jnp.cos(ang), jnp.sin(ang) * sign


def _layer_prompt(x, ple, w, *, tm_in=512, tm_merge=512, tm_ffn=512, ret_chunk=256, attn_tk=512):
    nb, seq, _ = x.shape
    n = nb * seq
    tm_in, tm_merge, tm_ffn = min(tm_in, seq), min(tm_merge, seq), min(tm_ffn, seq)
    ret_chunk, attn_tk = min(ret_chunk, seq), min(attn_tk, seq // 2)
    x2d = x.reshape(n, D_MODEL)
    cos, sin = _rope_tables(jnp.arange(seq))
    rqk, rv, rg, dq, dk, dkb, dv, dvb, ga, gb = _in_proj(
        x2d, w["g_mix"], w["w_in"], cos, sin, w["g_qn"], w["g_kn"], tm=tm_in)

    n_chunks = seq // ret_chunk
    s0 = jnp.zeros((nb, RET_HEADS, RET_DK, RET_DV), f32)
    oa, s_new = _retention(rqk, rv, rg, s0, chunk=ret_chunk, n_chunks=n_chunks,
                           row_index=lambda b, c: (b * n_chunks + c, 0))
    ob = _attn_prompt(dq, dkb, dvb, w["lam"], w["g_dnorm"], w["g_qn"], w["g_kn"],
                      nb=nb, seq=seq, tk=attn_tk)
    x1 = _merge(x2d, oa, ob, ga, gb, w["w_ret_br"], w["w_da_br"], w["w_out"], tm=tm_merge)

    cstate = jnp.zeros((nb, SUBLANES, 2 * D_FF), f32)
    y, ctail = _ffn(x1, ple.reshape(n, -1), cstate, w["g_ffn"], w["w_up"], w["conv_w"], w["conv_b"],
                    w["w_down"], w["g_ple"], w["w_ple_gate"], w["w_ple"], tm=tm_ffn, shift=1)
    return (y.reshape(nb, seq, D_MODEL),
            dk.reshape(nb, seq, DA_HEADS, 2, DA_DK),
            jnp.swapaxes(dv.reshape(nb, seq, 2, DA_HEADS, LANES), 2, 3).reshape(nb, seq, DA_HEADS, DA_DV),
            s_new, ctail[:, SUBLANES - (CONV_W - 1):])


def _layer_decode(x, ple, cache_k, cache_v, state_ret, state_conv, w, *, past_len):
    nb, t, _ = x.shape
    n = nb * t
    past = cache_k.shape[1]
    tm = n

    def to_tm(a):
        return jnp.swapaxes(a, 0, 1).reshape(n, a.shape[-1])

    def from_tm(a):
        return jnp.swapaxes(a.reshape(t, nb, a.shape[-1]), 0, 1)

    cos, sin = _rope_tables(past_len + jnp.arange(n) // nb)
    rqk, rv, rg, dq, dk, dkb, dv, dvb, ga, gb = _in_proj(
        to_tm(x), w["g_mix"], w["w_in"], cos, sin, w["g_qn"], w["g_kn"], tm=tm)

    view = lambda a: a.reshape(t, nb * IN_COLS)
    oa, s_new = _retention(view(rqk), view(rv), view(rg), state_ret, chunk=t, n_chunks=1,
                           row_index=lambda b, c: (0, b))
    ob = _attn_decode(view(dq), view(dkb), view(dvb),
                      cache_k.reshape(nb * past * DA_HEADS * 2, DA_DK),
                      cache_v.reshape(nb * past, DA_HEADS * DA_DV),
                      w["lam"], w["g_dnorm"], nb=nb, t=t, past=past)
    x1 = _merge(to_tm(x), oa.reshape(n, IN_COLS), ob.reshape(n, IN_COLS), ga, gb,
                w["w_ret_br"], w["w_da_br"], w["w_out"], tm=tm)

    halo = (CONV_W - 1) * nb
    cstate = jnp.swapaxes(state_conv, 0, 1).reshape(1, halo, 2 * D_FF)
    y, ctail = _ffn(x1, to_tm(ple), cstate, w["g_ffn"], w["w_up"], w["conv_w"], w["conv_b"],
                    w["w_down"], w["g_ple"], w["w_ple_gate"], w["w_ple"], tm=tm, shift=nb)
    return (from_tm(y),
            jnp.swapaxes(dk.reshape(t, nb, DA_HEADS, 2, DA_DK), 0, 1),
            jnp.transpose(dv.reshape(t, nb, 2, DA_HEADS, LANES), (1, 0, 3, 2, 4)).reshape(
                nb, t, DA_HEADS, DA_DV),
            s_new, jnp.swapaxes(ctail.reshape(CONV_W - 1, nb, 2 * D_FF), 0, 1))


def _layer_weights(i, g_mix, w_in, g_qn, g_kn, lam_q1, lam_k1, lam_q2, lam_k2, g_dnorm, w_ret_br,
                   w_da_br, w_out, g_ffn, w_up, conv_w, conv_b, w_down, g_ple, w_ple_gate, w_ple):
    vec = lambda a: a[i].reshape(1, -1)
    return dict(
        g_mix=vec(g_mix), w_in=w_in[i].astype(bf16), g_qn=vec(g_qn), g_kn=vec(g_kn),
        lam=(vec(lam_q1), vec(lam_k1), vec(lam_q2), vec(lam_k2)), g_dnorm=vec(g_dnorm),
        w_ret_br=w_ret_br[i].astype(bf16), w_da_br=w_da_br[i].astype(bf16),
        w_out=w_out[i].astype(bf16), g_ffn=vec(g_ffn), w_up=w_up[i].astype(bf16),
        conv_w=conv_w[i], conv_b=vec(conv_b), w_down=w_down[i].astype(bf16), g_ple=vec(g_ple),
        w_ple_gate=w_ple_gate[i].astype(bf16), w_ple=w_ple[i].astype(bf16))


def kernel(x_prompt, x_sample, cache_k, cache_v, state_ret, state_conv, p_prompt, p_sample, g_mix, w_in, g_qn, g_kn, lam_q1, lam_k1, lam_q2, lam_k2, g_dnorm, w_ret_br, w_da_br, w_out, g_ffn, w_up, conv_w, conv_b, w_down, g_ple, w_ple_gate, w_ple):
    depth = w_in.shape[0]
    yp, ys = x_prompt, x_sample
    outs = [[] for _ in range(8)]
    for i in range(depth):
        w = _layer_weights(i, g_mix, w_in, g_qn, g_kn, lam_q1, lam_k1, lam_q2, lam_k2, g_dnorm,
                           w_ret_br, w_da_br, w_out, g_ffn, w_up, conv_w, conv_b, w_down, g_ple,
                           w_ple_gate, w_ple)
        yp, k1, v1, s1, c1 = _layer_prompt(yp, p_prompt[i], w)
        ys, k2, v2, s2, c2 = _layer_decode(ys, p_sample[i], cache_k[i], cache_v[i], state_ret[i],
                                           state_conv[i], w, past_len=PAST_LEN)
        for lst, val in zip(outs, (k1, k2, v1, v2, s1, s2, c1, c2)):
            lst.append(val)
    return (yp, ys) + tuple(jnp.stack(o) for o in outs)
```

```python
import functools
import math

import jax
import jax.numpy as jnp
import numpy as np
from jax import lax
from jax.experimental import pallas as pl
from jax.experimental.pallas import tpu as pltpu

D_MODEL = 1024
CHUNK = 64
RET_HEADS = 4
RET_DK = 128
RET_DV = 256
DA_HEADS = 4
DA_DK = 128
DA_DV = 256
D_FF = 2816
CONV_W = 3
ROPE_THETA = 10000.0
EPS = 1e-6
PAST_LEN = 1024
LAM_INIT = 0.8 - 0.6 * math.exp(-0.3 * 0)
LOG2E = math.log2(math.e)

LANES = 128
SUBLANES = 8
IN_COLS = 1024
N_IN_BLOCKS = 8
VMEM_LIMIT = 56 * 1024 * 1024

f32 = jnp.float32
bf16 = jnp.bfloat16

NT_DIMS = (((1,), (1,)), ((), ()))
TN_DIMS = (((0,), (0,)), ((), ()))


def _rms(x):
    return x * lax.rsqrt(jnp.mean(x * x, axis=-1, keepdims=True) + EPS)


def _rope(x, cos, sin_signed):
    return x * cos + pltpu.roll(x, RET_DK // 2, 1) * sin_signed


def _sigmoid(x):
    return 0.5 * jnp.tanh(0.5 * x) + 0.5


def _silu(x):
    h = 0.5 * x
    return h * jnp.tanh(h) + h


def _params(sem):
    return pltpu.CompilerParams(dimension_semantics=sem, vmem_limit_bytes=VMEM_LIMIT)


def _const_spec(shape):
    nd = len(shape)
    return pl.BlockSpec(shape, lambda *_: (0,) * nd)


def _in_proj_kernel(x_ref, gmix_ref, w_ref, cos_ref, sin_ref, gq_ref, gk_ref,
                    rqk_ref, rv_ref, rg_ref, dq_ref, dk_ref, dkb_ref, dv_ref, dvb_ref,
                    ga_ref, gb_ref):
    h = (_rms(x_ref[...]) * gmix_ref[...]).astype(bf16)
    cos = cos_ref[...]
    sin = sin_ref[...]
    n_heads = IN_COLS // LANES

    def block(j):
        return jnp.dot(h, w_ref[:, j * IN_COLS:(j + 1) * IN_COLS], preferred_element_type=f32)

    def heads(z):
        return [(slice(g * LANES, (g + 1) * LANES), z[:, g * LANES:(g + 1) * LANES])
                for g in range(n_heads)]

    tm = x_ref.shape[0]
    for g, (sl, zg) in enumerate(heads(block(4))):
        r = _rope(_rms(zg) * gk_ref[...], cos, sin)
        dk_ref[pl.ds(g, tm, stride=n_heads), :] = r
        dkb_ref[:, sl] = r.astype(bf16)

    for sl, zg in heads(block(3)):
        r = _rope(_rms(zg) * gq_ref[...], cos, sin) * (DA_DK ** -0.5 * LOG2E)
        dq_ref[:, sl] = r.astype(bf16)

    for g, (sl, zg) in enumerate(heads(block(0))):
        r = _rope(zg, cos, sin)
        if g >= RET_HEADS:
            r = r * (RET_DK ** -0.5)
        rqk_ref[:, sl] = r.astype(bf16)

    z = block(5)
    for g, (sl, zg) in enumerate(heads(z)):
        head, half = divmod(g, 2)
        dv_ref[pl.ds(half * DA_HEADS + head, tm, stride=n_heads), :] = zg
    dvb_ref[...] = z.astype(bf16)
    z = block(2)
    rg_ref[...] = _silu(z).astype(bf16)
    ga_ref[...] = _sigmoid(block(6)).astype(bf16)
    gb_ref[...] = _sigmoid(block(7)).astype(bf16)
    rv_ref[...] = block(1).astype(bf16)


def _resident(shape):
    nd = len(shape)
    return pl.BlockSpec(shape, lambda *_: (0,) * nd, pipeline_mode=pl.Buffered(1))


def _in_proj(x2d, g_mix, w_in, cos, sin, g_qn, g_kn, *, tm):
    n = x2d.shape[0]
    pos_blocks = cos.shape[0] // tm
    row = pl.BlockSpec((tm, IN_COLS), lambda i: (i, 0))
    tab = pl.BlockSpec((tm, LANES), lambda i: (i % pos_blocks, 0))
    bf = jax.ShapeDtypeStruct((n, IN_COLS), bf16)
    chunks = IN_COLS // LANES
    fl = jax.ShapeDtypeStruct((n * chunks, LANES), f32)
    lin = pl.BlockSpec((tm * chunks, LANES), lambda i: (i, 0))
    return pl.pallas_call(
        _in_proj_kernel,
        grid=(n // tm,),
        in_specs=[row, _resident((1, D_MODEL)), _resident((D_MODEL, N_IN_BLOCKS * IN_COLS)),
                  tab, tab, _resident((1, LANES)), _resident((1, LANES))],
        out_specs=[row, row, row, row, lin, row, lin, row, row, row],
        out_shape=[bf, bf, bf, bf, fl, bf, fl, bf, bf, bf],
        compiler_params=_params(("parallel",)),
        name="in_proj",
    )(x2d, g_mix, w_in, cos, sin, g_qn, g_kn)


def _ret_log_gamma():
    return np.log(1.0 - np.exp2(-5.0 - np.arange(RET_HEADS, dtype=np.float64)))


def _ret_tables(c):
    log_g = jnp.asarray(_ret_log_gamma(), f32)[:, None, None]
    idx = jnp.arange(c, dtype=f32)
    diff = idx[:, None] - idx[None, :]
    decay = jnp.where(diff >= 0, jnp.exp(log_g * jnp.maximum(diff, 0.0)), 0.0)
    qd = jnp.broadcast_to(jnp.exp(log_g * (idx + 1.0)[None, :, None]), (RET_HEADS, c, RET_DV))
    kd = jnp.broadcast_to(jnp.exp(log_g * (c - 1.0 - idx)[None, :, None]), (RET_HEADS, c, RET_DK))
    return decay, qd, kd


RET_STREAMS = 4


def _retention_kernel(qk_ref, v_ref, g_ref, dec_ref, qd_ref, kd_ref, s0_ref, o_ref, s_ref, *,
                      chunk, stream_major):
    @pl.when(pl.program_id(1) == 0)
    def _():
        s_ref[...] = s0_ref[...]

    def cols(ref, s, lo, width):
        if stream_major:
            return ref[s, :, lo:lo + width]
        return ref[:, s * IN_COLS + lo:s * IN_COLS + lo + width]

    log_g = _ret_log_gamma()
    for s in range(s_ref.shape[0]):
        for h in range(RET_HEADS):
            q = cols(qk_ref, s, h * RET_DK, RET_DK)
            k = cols(qk_ref, s, (RET_HEADS + h) * RET_DK, RET_DK)
            v = cols(v_ref, s, h * RET_DV, RET_DV)
            state = s_ref[s, h]
            scores = lax.dot_general(q, k, NT_DIMS, preferred_element_type=f32) * dec_ref[h]
            intra = jnp.dot(scores.astype(bf16), v, preferred_element_type=f32)
            inter = jnp.dot(q, state.astype(bf16), preferred_element_type=f32) * qd_ref[h]
            kw = (k.astype(f32) * kd_ref[h]).astype(bf16)
            s_ref[s, h] = float(np.exp(chunk * log_g[h])) * state + lax.dot_general(
                kw, v, TN_DIMS, preferred_element_type=f32)
            o = _rms(intra + inter)
            gate = cols(g_ref, s, h * RET_DV, RET_DV).astype(f32)
            out = (o * gate).astype(bf16)
            if stream_major:
                o_ref[s, :, h * RET_DV:(h + 1) * RET_DV] = out
            else:
                o_ref[:, s * IN_COLS + h * RET_DV:s * IN_COLS + (h + 1) * RET_DV] = out


def _retention(rqk, rv, rg, s0, *, chunk, n_chunks, stream_major):
    nb = s0.shape[0]
    streams = math.gcd(nb, RET_STREAMS)
    decay, qd, kd = _ret_tables(chunk)
    if stream_major:
        blk = pl.BlockSpec((streams, chunk, IN_COLS), lambda b, c: (b, c, 0))
    else:
        blk = pl.BlockSpec((chunk, streams * IN_COLS), lambda b, c: (0, b))
    st = pl.BlockSpec((streams, RET_HEADS, RET_DK, RET_DV), lambda b, c: (b, 0, 0, 0))
    return pl.pallas_call(
        functools.partial(_retention_kernel, chunk=chunk, stream_major=stream_major),
        grid=(nb // streams, n_chunks),
        in_specs=[blk, blk, blk, _const_spec(decay.shape), _const_spec(qd.shape),
                  _const_spec(kd.shape), st],
        out_specs=[blk, st],
        out_shape=[jax.ShapeDtypeStruct(rqk.shape, bf16), jax.ShapeDtypeStruct(s0.shape, f32)],
        compiler_params=_params(("parallel", "arbitrary")),
        name="retention",
    )(rqk, rv, rg, decay, qd, kd, s0)


def _lam(lq1_ref, lk1_ref, lq2_ref, lk2_ref):
    e1 = jnp.exp(jnp.sum(lq1_ref[...] * lk1_ref[...], axis=-1, keepdims=True))
    e2 = jnp.exp(jnp.sum(lq2_ref[...] * lk2_ref[...], axis=-1, keepdims=True))
    return e1 - e2 + LAM_INIT


ATTN_SUB = 64
FAST_SCORE_MAX = 30.0
SCORE_BOUND_PER_GAIN = 1.02 * LOG2E * DA_DK ** 0.5


def _lane_tile(x, width):
    return jnp.concatenate([x] * (width // LANES), axis=1)


def _attn_prompt_kernel(q_ref, k_ref, v_ref, lq1_ref, lk1_ref, lq2_ref, lk2_ref, gd_ref, gq_ref, gk_ref,
                        dmask_ref, o_ref, m_ref, l_ref, acc_ref, sa_ref, sb_ref, alpha_ref, p_ref, *, tk):
    tq = 2 * tk
    qi = pl.program_id(2)

    gq_max = jnp.max(jnp.abs(gq_ref[...]), axis=1, keepdims=True)
    gk_max = jnp.max(jnp.abs(gk_ref[...]), axis=1, keepdims=True)
    fast = (gq_max * gk_max)[0, 0] * SCORE_BOUND_PER_GAIN <= FAST_SCORE_MAX

    def probs(kb, slot, diag=None, r0=0, first=False):
        start = pl.multiple_of(kb * tk, tk)
        for a in range(2):
            buf = 2 * slot + a
            q = q_ref[r0:, a * DA_DK:(a + 1) * DA_DK]
            k = k_ref[pl.ds(start, tk), a * DA_DK:(a + 1) * DA_DK]
            s = lax.dot_general(q, k, NT_DIMS, preferred_element_type=f32)
            if diag == 0:
                s = jnp.concatenate([s[:tk] + dmask_ref[...], s[tk:]], axis=0)
            elif diag == 1:
                s = s + dmask_ref[...]
            p = jnp.exp2(s)
            part = p[:, 0:LANES]
            for c in range(1, tk // LANES):
                part = part + p[:, c * LANES:(c + 1) * LANES]
            l_ref[a, r0:, :] = part if first else l_ref[a, r0:, :] + part
            p_ref[buf, r0:, :] = p.astype(bf16)

    def pv_add(kb, slot, r0=0):
        start = pl.multiple_of(kb * tk, tk)
        v = v_ref[pl.ds(start, tk), :]
        for a in range(2):
            acc_ref[a, r0:, :] = acc_ref[a, r0:, :] + jnp.dot(
                p_ref[2 * slot + a, r0:, :], v, preferred_element_type=f32)

    def fast_path():
        def pair(i, last_diag):
            pv_add(2 * i, 0)
            probs(2 * i + 1, 1)
            pv_add(2 * i + 1, 1)
            probs(2 * i + 2, 0, diag=last_diag)

        @pl.when(qi == 0)
        def _():
            probs(0, 0, diag=0, first=True)
            acc_ref[...] = jnp.zeros(acc_ref.shape, f32)

        @pl.when(qi > 0)
        def _():
            probs(0, 0, first=True)
            acc_ref[...] = jnp.zeros(acc_ref.shape, f32)

            def body(i, carry):
                pair(i, None)
                return carry

            lax.fori_loop(0, qi - 1, body, 0)
            pair(qi - 1, 0)

        pv_add(2 * qi, 0)
        probs(2 * qi + 1, 1, diag=1, r0=tk)
        pv_add(2 * qi + 1, 1, r0=tk)
        for a in range(2):
            l_ref[a] = jnp.broadcast_to(jnp.sum(l_ref[a], axis=1, keepdims=True), l_ref.shape[1:])


    def scores(kb, s_ref, r0=0):
        start = pl.multiple_of(kb * tk, tk)
        for a in range(2):
            q = q_ref[r0:, a * DA_DK:(a + 1) * DA_DK]
            k = k_ref[pl.ds(start, tk), a * DA_DK:(a + 1) * DA_DK]
            s_ref[a, r0:, :] = lax.dot_general(q, k, NT_DIMS, preferred_element_type=f32)

    def softmax(slot, diag=None, r0=0):
        s_ref = (sa_ref, sb_ref)[slot]
        for a in range(2):
            buf = 2 * slot + a
            for r in range(r0 // ATTN_SUB, tq // ATTN_SUB):
                rows = slice(r * ATTN_SUB, (r + 1) * ATTN_SUB)
                s = s_ref[a, rows, :]
                if diag is not None and (diag * tk + tk - 1) // CHUNK > (r * ATTN_SUB) // CHUNK:
                    row = (lax.broadcasted_iota(jnp.int32, s.shape, 0) + r * ATTN_SUB) // CHUNK
                    col = (lax.broadcasted_iota(jnp.int32, s.shape, 1) + diag * tk) // CHUNK
                    s = jnp.where(col <= row, s, -1e30)
                m_prev = m_ref[a, rows, :]
                m_new = jnp.maximum(m_prev, jnp.max(s, axis=1, keepdims=True))
                alpha = jnp.exp2(m_prev - m_new)
                p = jnp.exp2(s - _lane_tile(m_new, tk))
                l_ref[a, rows, :] = alpha * l_ref[a, rows, :] + jnp.sum(p, axis=1, keepdims=True)
                m_ref[a, rows, :] = m_new
                alpha_ref[buf, rows, :] = alpha
                p_ref[buf, rows, :] = p.astype(bf16)

    def pv(kb, slot, r0=0):
        start = pl.multiple_of(kb * tk, tk)
        v = v_ref[pl.ds(start, tk), :]
        for a in range(2):
            buf = 2 * slot + a
            acc_ref[a, r0:, :] = acc_ref[a, r0:, :] * _lane_tile(alpha_ref[buf, r0:, :], DA_DV) + jnp.dot(
                p_ref[buf, r0:, :], v, preferred_element_type=f32)

    def safe_path():
        m_ref[...] = jnp.full(m_ref.shape, -1e30, f32)
        l_ref[...] = jnp.zeros(l_ref.shape, f32)
        acc_ref[...] = jnp.zeros(acc_ref.shape, f32)
        alpha_ref[2:4] = jnp.ones((2,) + alpha_ref.shape[1:], f32)
        p_ref[2:4] = jnp.zeros((2,) + p_ref.shape[1:], bf16)
        scores(0, sa_ref)

        def body(i, carry):
            kb = 2 * i
            pv(jnp.maximum(kb - 1, 0), 1)
            scores(kb + 1, sb_ref)
            softmax(0)
            pv(kb, 0)
            scores(kb + 2, sa_ref)
            softmax(1)
            return carry

        lax.fori_loop(0, qi, body, 0)
        pv(jnp.maximum(2 * qi - 1, 0), 1)
        scores(2 * qi + 1, sb_ref, r0=tk)
        softmax(0, diag=0)
        pv(2 * qi, 0)
        softmax(1, diag=1, r0=tk)
        pv(2 * qi + 1, 1, r0=tk)

    pl.when(fast)(fast_path)
    pl.when(jnp.logical_not(fast))(safe_path)

    lam = _lam(lq1_ref, lk1_ref, lq2_ref, lk2_ref)
    o0 = acc_ref[0] * _lane_tile(1.0 / l_ref[0], DA_DV)
    o1 = acc_ref[1] * _lane_tile(1.0 / l_ref[1], DA_DV)
    o = _rms(o0 - lam * o1) * gd_ref[...] * (1.0 - LAM_INIT)
    o_ref[...] = o.astype(bf16)


def _attn_prompt(dq, dkb, dvb, lam_params, g_dnorm, g_qn, g_kn, *, nb, seq, tk):
    hw = 2 * DA_DK
    tq = 2 * tk
    qspec = pl.BlockSpec((tq, hw), lambda b, h, i: (b * (seq // tq) + i, h))
    kvspec = pl.BlockSpec((seq, hw), lambda b, h, i: (b, h))
    vec = _const_spec((1, DA_DK))
    chunk_of = jnp.arange(tk) // CHUNK
    dmask = jnp.where(chunk_of[None, :] <= chunk_of[:, None], 0.0, -1e30).astype(f32)
    return pl.pallas_call(
        functools.partial(_attn_prompt_kernel, tk=tk),
        grid=(nb, DA_HEADS, seq // tq),
        in_specs=[qspec, kvspec, kvspec, vec, vec, vec, vec, _const_spec((1, DA_DV)), vec, vec,
                  _const_spec((tk, tk))],
        out_specs=qspec,
        out_shape=jax.ShapeDtypeStruct(dq.shape, bf16),
        scratch_shapes=[pltpu.VMEM((2, tq, LANES), f32), pltpu.VMEM((2, tq, LANES), f32),
                        pltpu.VMEM((2, tq, DA_DV), f32), pltpu.VMEM((2, tq, tk), f32),
                        pltpu.VMEM((2, tq, tk), f32), pltpu.VMEM((4, tq, LANES), f32),
                        pltpu.VMEM((4, tq, tk), bf16)],
        compiler_params=_params(("parallel", "parallel", "arbitrary")),
        name="attn_prompt",
    )(dq, dkb, dvb, *lam_params, g_dnorm, g_qn, g_kn, dmask)


def _attn_decode_kernel(q_ref, kc_ref, vc_ref, kn_ref, vn_ref, lq1_ref, lk1_ref, lq2_ref, lk2_ref,
                        gd_ref, o_ref, *, past):
    lam = _lam(lq1_ref, lk1_ref, lq2_ref, lk2_ref)
    rows_per_token = DA_HEADS * 2
    for h in range(DA_HEADS):
        hs = slice(h * DA_DV, (h + 1) * DA_DV)
        probs = []
        for a in range(2):
            cols = slice(h * DA_DV + a * DA_DK, h * DA_DV + (a + 1) * DA_DK)
            q = q_ref[:, cols]
            kc = kc_ref[pl.ds(h * 2 + a, past, stride=rows_per_token), :].astype(bf16)
            kn = kn_ref[:, cols]
            sc = lax.dot_general(q, kc, NT_DIMS, preferred_element_type=f32)
            sn = lax.dot_general(q, kn, NT_DIMS, preferred_element_type=f32)
            m = jnp.maximum(jnp.max(sc, axis=1, keepdims=True), jnp.max(sn, axis=1, keepdims=True))
            pc = jnp.exp2(sc - m)
            pn = jnp.exp2(sn - m)
            inv = 1.0 / (jnp.sum(pc, axis=1, keepdims=True) + jnp.sum(pn, axis=1, keepdims=True))
            probs.append((pc * inv, pn * inv))
        ac = probs[0][0] - lam * probs[1][0]
        an = probs[0][1] - lam * probs[1][1]
        o = jnp.dot(ac.astype(bf16), vc_ref[:, hs].astype(bf16), preferred_element_type=f32)
        o = o + jnp.dot(an.astype(bf16), vn_ref[:, hs], preferred_element_type=f32)
        o_ref[:, hs] = (_rms(o) * gd_ref[...] * (1.0 - LAM_INIT)).astype(bf16)


def _attn_decode(dq_v, dkb_v, dvb_v, cache_k_rows, cache_v2d, lam_params, g_dnorm, *, nb, t, past):
    new = pl.BlockSpec((t, IN_COLS), lambda b: (0, b))
    old_k = pl.BlockSpec((past * DA_HEADS * 2, DA_DK), lambda b: (b, 0))
    old_v = pl.BlockSpec((past, IN_COLS), lambda b: (b, 0))
    vec = _const_spec((1, DA_DK))
    return pl.pallas_call(
        functools.partial(_attn_decode_kernel, past=past),
        grid=(nb,),
        in_specs=[new, old_k, old_v, new, new, vec, vec, vec, vec, _const_spec((1, DA_DV))],
        out_specs=new,
        out_shape=jax.ShapeDtypeStruct(dq_v.shape, bf16),
        compiler_params=_params(("parallel",)),
        name="attn_decode",
    )(dq_v, cache_k_rows, cache_v2d, dkb_v, dvb_v, *lam_params, g_dnorm)


def _merge_kernel(x_ref, oa_ref, ob_ref, ga_ref, gb_ref, wr_ref, wd_ref, wo_ref, y_ref):
    a = jnp.dot(oa_ref[...], wr_ref[...], preferred_element_type=f32)
    b = jnp.dot(ob_ref[...], wd_ref[...], preferred_element_type=f32)
    m = ga_ref[...].astype(f32) * a + gb_ref[...].astype(f32) * b
    y_ref[...] = x_ref[...] + jnp.dot(m.astype(bf16), wo_ref[...], preferred_element_type=f32)


def _merge(x2d, oa, ob, ga, gb, w_ret_br, w_da_br, w_out, *, tm):
    n = x2d.shape[0]
    row = pl.BlockSpec((tm, D_MODEL), lambda i: (i, 0))
    w = _const_spec((D_MODEL, D_MODEL))
    return pl.pallas_call(
        _merge_kernel,
        grid=(n // tm,),
        in_specs=[row] * 5 + [w] * 3,
        out_specs=row,
        out_shape=jax.ShapeDtypeStruct(x2d.shape, f32),
        compiler_params=_params(("parallel",)),
        name="merge",
    )(x2d, oa, ob, ga, gb, w_ret_br, w_da_br, w_out)


FF_CHUNK = 256


def _ffn_kernel(x_ref, ple_ref, cst_ref, gffn_ref, wup_ref, cw_ref, cb_ref, wdn_ref, gple_ref,
                wpg_ref, wple_ref, y_ref, cnew_ref, uext_ref, act_ref, *, tm, shift, halo):
    @pl.when(pl.program_id(1) == 0)
    def _():
        uext_ref[0:halo, :] = cst_ref[0]

    x1 = x_ref[...]
    xn = (_rms(x1) * gffn_ref[...]).astype(bf16)
    uext_ref[halo:halo + tm, :] = jnp.dot(xn, wup_ref[...], preferred_element_type=f32)

    def conv(col):
        sl = slice(col, col + FF_CHUNK)
        acc = cw_ref[0:1, sl] * uext_ref[halo - 2 * shift:halo - 2 * shift + tm, sl]
        acc = acc + cw_ref[1:2, sl] * uext_ref[halo - shift:halo - shift + tm, sl]
        acc = acc + cw_ref[2:3, sl] * uext_ref[halo:halo + tm, sl]
        return acc + cb_ref[:, sl]

    for c in range(D_FF // FF_CHUNK):
        cg = conv(c * FF_CHUNK)
        cv = conv(D_FF + c * FF_CHUNK)
        act_ref[:, c * FF_CHUNK:(c + 1) * FF_CHUNK] = (_silu(cg) * cv).astype(bf16)

    x2 = x1 + jnp.dot(act_ref[...], wdn_ref[...], preferred_element_type=f32)
    tail = uext_ref[tm:tm + halo, :]
    cnew_ref[0] = tail
    uext_ref[0:halo, :] = tail

    gate = jnp.dot((_rms(x2) * gple_ref[...]).astype(bf16), wpg_ref[...], preferred_element_type=f32)
    emb = jnp.dot(ple_ref[...].astype(bf16), wple_ref[...], preferred_element_type=f32)
    y_ref[...] = x2 + _sigmoid(gate) * emb


def _ffn(x1, ple2d, cstate, g_ffn, w_up, conv_w, conv_b, w_down, g_ple, w_ple_gate, w_ple, *,
         tm, shift):
    n = x1.shape[0]
    nseg, halo, _ = cstate.shape
    tiles = n // (nseg * tm)
    ple_dim = ple2d.shape[1]
    resident = _resident
    row = lambda width: pl.BlockSpec((tm, width), lambda s, t: (s * tiles + t, 0))
    seg = pl.BlockSpec((1, halo, 2 * D_FF), lambda s, t: (s, 0, 0))
    return pl.pallas_call(
        functools.partial(_ffn_kernel, tm=tm, shift=shift, halo=halo),
        grid=(nseg, tiles),
        in_specs=[row(D_MODEL), row(ple_dim), seg, resident((1, D_MODEL)),
                  resident((D_MODEL, 2 * D_FF)), resident((CONV_W, 2 * D_FF)),
                  resident((1, 2 * D_FF)), resident((D_FF, D_MODEL)), resident((1, D_MODEL)),
                  resident((D_MODEL, D_MODEL)), resident((ple_dim, D_MODEL))],
        out_specs=[row(D_MODEL), seg],
        out_shape=[jax.ShapeDtypeStruct(x1.shape, f32), jax.ShapeDtypeStruct(cstate.shape, f32)],
        scratch_shapes=[pltpu.VMEM((halo + tm, 2 * D_FF), f32), pltpu.VMEM((tm, D_FF), bf16)],
        compiler_params=_params(("parallel", "arbitrary")),
        name="ffn",
    )(x1, ple2d, cstate, g_ffn, w_up, conv_w, conv_b, w_down, g_ple, w_ple_gate, w_ple)


def _rope_tables(pos):
    inv = ROPE_THETA ** (-jnp.arange(0, RET_DK, 2, dtype=f32) / RET_DK)
    ang = pos.astype(f32)[:, None] * inv[None, :]
    ang = jnp.concatenate([ang, ang], axis=-1)
    sign = jnp.where(jnp.arange(RET_DK) < RET_DK // 2, -1.0, 1.0).astype(f32)
    return jnp.cos(ang), jnp.sin(ang) * sign


def _layer_prompt(x, ple, w, *, tm_in=512, tm_merge=512, tm_ffn=512, ret_chunk=256, attn_tk=512):
    nb, seq, _ = x.shape
    n = nb * seq
    tm_in, tm_merge, tm_ffn = min(tm_in, seq), min(tm_merge, seq), min(tm_ffn, seq)
    ret_chunk, attn_tk = min(ret_chunk, seq), min(attn_tk, seq // 2)
    x2d = x.reshape(n, D_MODEL)
    cos, sin = _rope_tables(jnp.arange(seq))
    rqk, rv, rg, dq, dk, dkb, dv, dvb, ga, gb = _in_proj(
        x2d, w["g_mix"], w["w_in"], cos, sin, w["g_qn"], w["g_kn"], tm=tm_in)

    n_chunks = seq // ret_chunk
    s0 = jnp.zeros((nb, RET_HEADS, RET_DK, RET_DV), f32)
    per_stream = lambda a: a.reshape(nb, seq, IN_COLS)
    oa, s_new = _retention(per_stream(rqk), per_stream(rv), per_stream(rg), s0, chunk=ret_chunk,
                           n_chunks=n_chunks, stream_major=True)
    oa = oa.reshape(n, IN_COLS)
    ob = _attn_prompt(dq, dkb, dvb, w["lam"], w["g_dnorm"], w["g_qn"], w["g_kn"],
                      nb=nb, seq=seq, tk=attn_tk)
    x1 = _merge(x2d, oa, ob, ga, gb, w["w_ret_br"], w["w_da_br"], w["w_out"], tm=tm_merge)

    cstate = jnp.zeros((nb, SUBLANES, 2 * D_FF), f32)
    y, ctail = _ffn(x1, ple.reshape(n, -1), cstate, w["g_ffn"], w["w_up"], w["conv_w"], w["conv_b"],
                    w["w_down"], w["g_ple"], w["w_ple_gate"], w["w_ple"], tm=tm_ffn, shift=1)
    return (y.reshape(nb, seq, D_MODEL),
            dk.reshape(nb, seq, DA_HEADS, 2, DA_DK),
            jnp.swapaxes(dv.reshape(nb, seq, 2, DA_HEADS, LANES), 2, 3).reshape(nb, seq, DA_HEADS, DA_DV),
            s_new, ctail[:, SUBLANES - (CONV_W - 1):])


def _layer_decode(x, ple, cache_k, cache_v, state_ret, state_conv, w, *, past_len):
    nb, t, _ = x.shape
    n = nb * t
    past = cache_k.shape[1]
    tm = n

    def to_tm(a):
        return jnp.swapaxes(a, 0, 1).reshape(n, a.shape[-1])

    def from_tm(a):
        return jnp.swapaxes(a.reshape(t, nb, a.shape[-1]), 0, 1)

    cos, sin = _rope_tables(past_len + jnp.arange(n) // nb)
    rqk, rv, rg, dq, dk, dkb, dv, dvb, ga, gb = _in_proj(
        to_tm(x), w["g_mix"], w["w_in"], cos, sin, w["g_qn"], w["g_kn"], tm=tm)

    view = lambda a: a.reshape(t, nb * IN_COLS)
    oa, s_new = _retention(view(rqk), view(rv), view(rg), state_ret, chunk=t, n_chunks=1,
                           stream_major=False)
    ob = _attn_decode(view(dq), view(dkb), view(dvb),
                      cache_k.reshape(nb * past * DA_HEADS * 2, DA_DK),
                      cache_v.reshape(nb * past, DA_HEADS * DA_DV),
                      w["lam"], w["g_dnorm"], nb=nb, t=t, past=past)
    x1 = _merge(to_tm(x), oa.reshape(n, IN_COLS), ob.reshape(n, IN_COLS), ga, gb,
                w["w_ret_br"], w["w_da_br"], w["w_out"], tm=tm)

    halo = (CONV_W - 1) * nb
    cstate = jnp.swapaxes(state_conv, 0, 1).reshape(1, halo, 2 * D_FF)
    y, ctail = _ffn(x1, to_tm(ple), cstate, w["g_ffn"], w["w_up"], w["conv_w"], w["conv_b"],
                    w["w_down"], w["g_ple"], w["w_ple_gate"], w["w_ple"], tm=tm, shift=nb)
    return (from_tm(y),
            jnp.swapaxes(dk.reshape(t, nb, DA_HEADS, 2, DA_DK), 0, 1),
            jnp.transpose(dv.reshape(t, nb, 2, DA_HEADS, LANES), (1, 0, 3, 2, 4)).reshape(
                nb, t, DA_HEADS, DA_DV),
            s_new, jnp.swapaxes(ctail.reshape(CONV_W - 1, nb, 2 * D_FF), 0, 1))


CAST_STEPS = 8


def _cast_kernel(*refs):
    n = len(refs) // 2
    for src, dst in zip(refs[:n], refs[n:]):
        dst[...] = src[...].astype(bf16)


def _cast_weights(layer, mats):
    names = list(mats)
    shapes = [mats[k].shape[1:] for k in names]
    spec = lambda sq: [pl.BlockSpec(((None,) if sq else ()) + (r // CAST_STEPS, c),
                                    (lambda j: (layer, j, 0)) if sq else (lambda j: (j, 0)))
                       for r, c in shapes]
    outs = pl.pallas_call(
        _cast_kernel,
        grid=(CAST_STEPS,),
        in_specs=spec(True),
        out_specs=spec(False),
        out_shape=[jax.ShapeDtypeStruct(s, bf16) for s in shapes],
        compiler_params=_params(("parallel",)),
        name="cast_weights",
    )(*[mats[k] for k in names])
    return dict(zip(names, outs))


def _layer_weights(i, g_mix, w_in, g_qn, g_kn, lam_q1, lam_k1, lam_q2, lam_k2, g_dnorm, w_ret_br,
                   w_da_br, w_out, g_ffn, w_up, conv_w, conv_b, w_down, g_ple, w_ple_gate, w_ple):
    vec = lambda a: a[i].reshape(1, -1)
    mats = _cast_weights(i, dict(w_in=w_in, w_ret_br=w_ret_br, w_da_br=w_da_br, w_out=w_out, w_up=w_up,
                                 w_down=w_down, w_ple_gate=w_ple_gate, w_ple=w_ple))
    return dict(
        g_mix=vec(g_mix), g_qn=vec(g_qn), g_kn=vec(g_kn),
        lam=(vec(lam_q1), vec(lam_k1), vec(lam_q2), vec(lam_k2)), g_dnorm=vec(g_dnorm),
        g_ffn=vec(g_ffn), conv_w=conv_w[i], conv_b=vec(conv_b), g_ple=vec(g_ple), **mats)


def kernel(x_prompt, x_sample, cache_k, cache_v, state_ret, state_conv, p_prompt, p_sample, g_mix, w_in, g_qn, g_kn, lam_q1, lam_k1, lam_q2, lam_k2, g_dnorm, w_ret_br, w_da_br, w_out, g_ffn, w_up, conv_w, conv_b, w_down, g_ple, w_ple_gate, w_ple):
    depth = w_in.shape[0]
    yp, ys = x_prompt, x_sample
    outs = [[] for _ in range(8)]
    for i in range(depth):
        w = _layer_weights(i, g_mix, w_in, g_qn, g_kn, lam_q1, lam_k1, lam_q2, lam_k2, g_dnorm,
                           w_ret_br, w_da_br, w_out, g_ffn, w_up, conv_w, conv_b, w_down, g_ple,
                           w_ple_gate, w_ple)
        yp, k1, v1, s1, c1 = _layer_prompt(yp, p_prompt[i], w)
        ys, k2, v2, s2, c2 = _layer_decode(ys, p_sample[i], cache_k[i], cache_v[i], state_ret[i],
                                           state_conv[i], w, past_len=PAST_LEN)
        for lst, val in zip(outs, (k1, k2, v1, v2, s1, s2, c1, c2)):
            lst.append(val)
    return (yp, ys) + tuple(jnp.stack(o) for o in outs)
```

```python
import functools
import math

import jax
import jax.numpy as jnp
import numpy as np
from jax import lax
from jax.experimental import pallas as pl
from jax.experimental.pallas import tpu as pltpu

D_MODEL = 1024
CHUNK = 64
RET_HEADS = 4
RET_DK = 128
RET_DV = 256
DA_HEADS = 4
DA_DK = 128
DA_DV = 256
D_FF = 2816
CONV_W = 3
ROPE_THETA = 10000.0
EPS = 1e-6
PAST_LEN = 1024
LAM_INIT = 0.8 - 0.6 * math.exp(-0.3 * 0)
LOG2E = math.log2(math.e)

LANES = 128
SUBLANES = 8
IN_COLS = 1024
N_IN_BLOCKS = 8
VMEM_LIMIT = 56 * 1024 * 1024

f32 = jnp.float32
bf16 = jnp.bfloat16

NT_DIMS = (((1,), (1,)), ((), ()))
TN_DIMS = (((0,), (0,)), ((), ()))


def _rms(x):
    return x * lax.rsqrt(jnp.mean(x * x, axis=-1, keepdims=True) + EPS)


def _rope(x, cos, sin_signed):
    return x * cos + pltpu.roll(x, RET_DK // 2, 1) * sin_signed


def _sigmoid(x):
    return 0.5 * jnp.tanh(0.5 * x) + 0.5


def _silu(x):
    h = 0.5 * x
    return h * jnp.tanh(h) + h


def _params(sem):
    return pltpu.CompilerParams(dimension_semantics=sem, vmem_limit_bytes=VMEM_LIMIT)


def _const_spec(shape):
    nd = len(shape)
    return pl.BlockSpec(shape, lambda *_: (0,) * nd)


def _in_proj_kernel(x_ref, gmix_ref, w_ref, cos_ref, sin_ref, gq_ref, gk_ref,
                    rqk_ref, rv_ref, rg_ref, dq_ref, dk_ref, dkb_ref, dv_ref, dvb_ref,
                    ga_ref, gb_ref):
    half = x_ref.shape[0] // 2
    h_top = (_rms(x_ref[0:half, :]) * gmix_ref[...]).astype(bf16)
    h_bot = (_rms(x_ref[half:, :]) * gmix_ref[...]).astype(bf16)
    h = jnp.concatenate([h_top, h_bot], axis=0)
    cos = cos_ref[...]
    sin = sin_ref[...]
    n_heads = IN_COLS // LANES

    def block(j):
        return jnp.dot(h, w_ref[:, j * IN_COLS:(j + 1) * IN_COLS], preferred_element_type=f32)

    def heads(z):
        return [(slice(g * LANES, (g + 1) * LANES), z[:, g * LANES:(g + 1) * LANES])
                for g in range(n_heads)]

    tm = x_ref.shape[0]
    w_dk = w_ref[:, 4 * IN_COLS:5 * IN_COLS]
    z_dk = jnp.concatenate([jnp.dot(h_top, w_dk, preferred_element_type=f32),
                            jnp.dot(h_bot, w_dk, preferred_element_type=f32)], axis=0)
    for g, (sl, zg) in enumerate(heads(z_dk)):
        r = _rope(_rms(zg) * gk_ref[...], cos, sin)
        dk_ref[pl.ds(g, tm, stride=n_heads), :] = r
        dkb_ref[:, sl] = r.astype(bf16)

    for sl, zg in heads(block(3)):
        r = _rope(_rms(zg) * gq_ref[...], cos, sin) * (DA_DK ** -0.5 * LOG2E)
        dq_ref[:, sl] = r.astype(bf16)

    for g, (sl, zg) in enumerate(heads(block(0))):
        r = _rope(zg, cos, sin)
        if g >= RET_HEADS:
            r = r * (RET_DK ** -0.5)
        rqk_ref[:, sl] = r.astype(bf16)

    z = block(5)
    for g, (sl, zg) in enumerate(heads(z)):
        head, half = divmod(g, 2)
        dv_ref[pl.ds(half * DA_HEADS + head, tm, stride=n_heads), :] = zg
    dvb_ref[...] = z.astype(bf16)
    z = block(2)
    rg_ref[...] = _silu(z).astype(bf16)
    ga_ref[...] = _sigmoid(block(6)).astype(bf16)
    gb_ref[...] = _sigmoid(block(7)).astype(bf16)
    rv_ref[...] = block(1).astype(bf16)


def _resident(shape):
    nd = len(shape)
    return pl.BlockSpec(shape, lambda *_: (0,) * nd, pipeline_mode=pl.Buffered(1))


def _in_proj(x2d, g_mix, w_in, cos, sin, g_qn, g_kn, *, tm):
    n = x2d.shape[0]
    pos_blocks = cos.shape[0] // tm
    row = pl.BlockSpec((tm, IN_COLS), lambda i: (i, 0))
    tab = pl.BlockSpec((tm, LANES), lambda i: (i % pos_blocks, 0))
    bf = jax.ShapeDtypeStruct((n, IN_COLS), bf16)
    chunks = IN_COLS // LANES
    fl = jax.ShapeDtypeStruct((n * chunks, LANES), f32)
    lin = pl.BlockSpec((tm * chunks, LANES), lambda i: (i, 0))
    return pl.pallas_call(
        _in_proj_kernel,
        grid=(n // tm,),
        in_specs=[row, _resident((1, D_MODEL)), _resident((D_MODEL, N_IN_BLOCKS * IN_COLS)),
                  tab, tab, _resident((1, LANES)), _resident((1, LANES))],
        out_specs=[row, row, row, row, lin, row, lin, row, row, row],
        out_shape=[bf, bf, bf, bf, fl, bf, fl, bf, bf, bf],
        compiler_params=_params(("parallel",)),
        name="in_proj",
    )(x2d, g_mix, w_in, cos, sin, g_qn, g_kn)


def _ret_log_gamma():
    return np.log(1.0 - np.exp2(-5.0 - np.arange(RET_HEADS, dtype=np.float64)))


def _ret_tables(c):
    log_g = jnp.asarray(_ret_log_gamma(), f32)[:, None, None]
    idx = jnp.arange(c, dtype=f32)
    diff = idx[:, None] - idx[None, :]
    decay = jnp.where(diff >= 0, jnp.exp(log_g * jnp.maximum(diff, 0.0)), 0.0)
    qd = jnp.broadcast_to(jnp.exp(log_g * (idx + 1.0)[None, :, None]), (RET_HEADS, c, RET_DV))
    kd = jnp.broadcast_to(jnp.exp(log_g * (c - 1.0 - idx)[None, :, None]), (RET_HEADS, c, RET_DK))
    return decay, qd, kd


RET_STREAMS = 4


def _retention_kernel(qk_ref, v_ref, g_ref, dec_ref, qd_ref, kd_ref, s0_ref, o_ref, s_ref, *,
                      chunk, stream_major):
    @pl.when(pl.program_id(1) == 0)
    def _():
        s_ref[...] = s0_ref[...]

    def cols(ref, s, lo, width):
        if stream_major:
            return ref[s, :, lo:lo + width]
        return ref[:, s * IN_COLS + lo:s * IN_COLS + lo + width]

    log_g = _ret_log_gamma()
    for s in range(s_ref.shape[0]):
        for h in range(RET_HEADS):
            q = cols(qk_ref, s, h * RET_DK, RET_DK)
            k = cols(qk_ref, s, (RET_HEADS + h) * RET_DK, RET_DK)
            v = cols(v_ref, s, h * RET_DV, RET_DV)
            state = s_ref[s, h]
            scores = lax.dot_general(q, k, NT_DIMS, preferred_element_type=f32) * dec_ref[h]
            intra = jnp.dot(scores.astype(bf16), v, preferred_element_type=f32)
            inter = jnp.dot(q, state.astype(bf16), preferred_element_type=f32) * qd_ref[h]
            kw = (k.astype(f32) * kd_ref[h]).astype(bf16)
            s_ref[s, h] = float(np.exp(chunk * log_g[h])) * state + lax.dot_general(
                kw, v, TN_DIMS, preferred_element_type=f32)
            o = _rms(intra + inter)
            gate = cols(g_ref, s, h * RET_DV, RET_DV).astype(f32)
            out = (o * gate).astype(bf16)
            if stream_major:
                o_ref[s, :, h * RET_DV:(h + 1) * RET_DV] = out
            else:
                o_ref[:, s * IN_COLS + h * RET_DV:s * IN_COLS + (h + 1) * RET_DV] = out


def _retention(rqk, rv, rg, s0, *, chunk, n_chunks, stream_major):
    nb = s0.shape[0]
    streams = math.gcd(nb, RET_STREAMS)
    decay, qd, kd = _ret_tables(chunk)
    if stream_major:
        blk = pl.BlockSpec((streams, chunk, IN_COLS), lambda b, c: (b, c, 0))
    else:
        blk = pl.BlockSpec((chunk, streams * IN_COLS), lambda b, c: (0, b))
    st = pl.BlockSpec((streams, RET_HEADS, RET_DK, RET_DV), lambda b, c: (b, 0, 0, 0))
    return pl.pallas_call(
        functools.partial(_retention_kernel, chunk=chunk, stream_major=stream_major),
        grid=(nb // streams, n_chunks),
        in_specs=[blk, blk, blk, _const_spec(decay.shape), _const_spec(qd.shape),
                  _const_spec(kd.shape), st],
        out_specs=[blk, st],
        out_shape=[jax.ShapeDtypeStruct(rqk.shape, bf16), jax.ShapeDtypeStruct(s0.shape, f32)],
        compiler_params=_params(("parallel", "arbitrary")),
        name="retention",
    )(rqk, rv, rg, decay, qd, kd, s0)


def _lam(lq1_ref, lk1_ref, lq2_ref, lk2_ref):
    e1 = jnp.exp(jnp.sum(lq1_ref[...] * lk1_ref[...], axis=-1, keepdims=True))
    e2 = jnp.exp(jnp.sum(lq2_ref[...] * lk2_ref[...], axis=-1, keepdims=True))
    return e1 - e2 + LAM_INIT


ATTN_SUB = 64
FAST_SCORE_MAX = 30.0
SCORE_BOUND_PER_GAIN = 1.02 * LOG2E * DA_DK ** 0.5


def _lane_tile(x, width):
    return jnp.concatenate([x] * (width // LANES), axis=1)


def _attn_prompt_kernel(q_ref, k_ref, v_ref, lq1_ref, lk1_ref, lq2_ref, lk2_ref, gd_ref, gq_ref, gk_ref,
                        dmask_ref, o_ref, m_ref, l_ref, acc_ref, sa_ref, sb_ref, alpha_ref, p_ref, *, tk):
    tq = 2 * tk
    qi = pl.program_id(2)

    gq_max = jnp.max(jnp.abs(gq_ref[...]), axis=1, keepdims=True)
    gk_max = jnp.max(jnp.abs(gk_ref[...]), axis=1, keepdims=True)
    fast = (gq_max * gk_max)[0, 0] * SCORE_BOUND_PER_GAIN <= FAST_SCORE_MAX

    def probs(kb, slot, diag=None, r0=0, first=False):
        start = pl.multiple_of(kb * tk, tk)
        for a in range(2):
            buf = 2 * slot + a
            q = q_ref[r0:, a * DA_DK:(a + 1) * DA_DK]
            k = k_ref[pl.ds(start, tk), a * DA_DK:(a + 1) * DA_DK]
            s = lax.dot_general(q, k, NT_DIMS, preferred_element_type=f32)
            if diag == 0:
                s = jnp.concatenate([s[:tk] + dmask_ref[...], s[tk:]], axis=0)
            elif diag == 1:
                s = s + dmask_ref[...]
            p = jnp.exp2(s)
            part = p[:, 0:LANES]
            for c in range(1, tk // LANES):
                part = part + p[:, c * LANES:(c + 1) * LANES]
            l_ref[a, r0:, :] = part if first else l_ref[a, r0:, :] + part
            p_ref[buf, r0:, :] = p.astype(bf16)

    def pv_add(kb, slot, r0=0):
        start = pl.multiple_of(kb * tk, tk)
        v = v_ref[pl.ds(start, tk), :]
        for a in range(2):
            acc_ref[a, r0:, :] = acc_ref[a, r0:, :] + jnp.dot(
                p_ref[2 * slot + a, r0:, :], v, preferred_element_type=f32)

    def fast_path():
        def pair(i, last_diag):
            pv_add(2 * i, 0)
            probs(2 * i + 1, 1)
            pv_add(2 * i + 1, 1)
            probs(2 * i + 2, 0, diag=last_diag)

        @pl.when(qi == 0)
        def _():
            probs(0, 0, diag=0, first=True)
            acc_ref[...] = jnp.zeros(acc_ref.shape, f32)

        @pl.when(qi > 0)
        def _():
            probs(0, 0, first=True)
            acc_ref[...] = jnp.zeros(acc_ref.shape, f32)

            def body(i, carry):
                pair(2 * i, None)
                pair(2 * i + 1, None)
                return carry

            n_pairs = qi - 1
            lax.fori_loop(0, lax.shift_right_logical(n_pairs, 1), body, 0)

            @pl.when(lax.bitwise_and(n_pairs, 1) == 1)
            def _():
                pair(n_pairs - 1, None)

            pair(qi - 1, 0)

        pv_add(2 * qi, 0)
        probs(2 * qi + 1, 1, diag=1, r0=tk)
        pv_add(2 * qi + 1, 1, r0=tk)
        for a in range(2):
            l_ref[a] = jnp.broadcast_to(jnp.sum(l_ref[a], axis=1, keepdims=True), l_ref.shape[1:])


    def scores(kb, s_ref, r0=0):
        start = pl.multiple_of(kb * tk, tk)
        for a in range(2):
            q = q_ref[r0:, a * DA_DK:(a + 1) * DA_DK]
            k = k_ref[pl.ds(start, tk), a * DA_DK:(a + 1) * DA_DK]
            s_ref[a, r0:, :] = lax.dot_general(q, k, NT_DIMS, preferred_element_type=f32)

    def softmax(slot, diag=None, r0=0):
        s_ref = (sa_ref, sb_ref)[slot]
        for a in range(2):
            buf = 2 * slot + a
            for r in range(r0 // ATTN_SUB, tq // ATTN_SUB):
                rows = slice(r * ATTN_SUB, (r + 1) * ATTN_SUB)
                s = s_ref[a, rows, :]
                if diag is not None and (diag * tk + tk - 1) // CHUNK > (r * ATTN_SUB) // CHUNK:
                    row = (lax.broadcasted_iota(jnp.int32, s.shape, 0) + r * ATTN_SUB) // CHUNK
                    col = (lax.broadcasted_iota(jnp.int32, s.shape, 1) + diag * tk) // CHUNK
                    s = jnp.where(col <= row, s, -1e30)
                m_prev = m_ref[a, rows, :]
                m_new = jnp.maximum(m_prev, jnp.max(s, axis=1, keepdims=True))
                alpha = jnp.exp2(m_prev - m_new)
                p = jnp.exp2(s - _lane_tile(m_new, tk))
                l_ref[a, rows, :] = alpha * l_ref[a, rows, :] + jnp.sum(p, axis=1, keepdims=True)
                m_ref[a, rows, :] = m_new
                alpha_ref[buf, rows, :] = alpha
                p_ref[buf, rows, :] = p.astype(bf16)

    def pv(kb, slot, r0=0):
        start = pl.multiple_of(kb * tk, tk)
        v = v_ref[pl.ds(start, tk), :]
        for a in range(2):
            buf = 2 * slot + a
            acc_ref[a, r0:, :] = acc_ref[a, r0:, :] * _lane_tile(alpha_ref[buf, r0:, :], DA_DV) + jnp.dot(
                p_ref[buf, r0:, :], v, preferred_element_type=f32)

    def safe_path():
        m_ref[...] = jnp.full(m_ref.shape, -1e30, f32)
        l_ref[...] = jnp.zeros(l_ref.shape, f32)
        acc_ref[...] = jnp.zeros(acc_ref.shape, f32)
        alpha_ref[2:4] = jnp.ones((2,) + alpha_ref.shape[1:], f32)
        p_ref[2:4] = jnp.zeros((2,) + p_ref.shape[1:], bf16)
        scores(0, sa_ref)

        def body(i, carry):
            kb = 2 * i
            pv(jnp.maximum(kb - 1, 0), 1)
            scores(kb + 1, sb_ref)
            softmax(0)
            pv(kb, 0)
            scores(kb + 2, sa_ref)
            softmax(1)
            return carry

        lax.fori_loop(0, qi, body, 0)
        pv(jnp.maximum(2 * qi - 1, 0), 1)
        scores(2 * qi + 1, sb_ref, r0=tk)
        softmax(0, diag=0)
        pv(2 * qi, 0)
        softmax(1, diag=1, r0=tk)
        pv(2 * qi + 1, 1, r0=tk)

    pl.when(fast)(fast_path)
    pl.when(jnp.logical_not(fast))(safe_path)

    lam = _lam(lq1_ref, lk1_ref, lq2_ref, lk2_ref)
    o = acc_ref[0] * _lane_tile(1.0 / l_ref[0], DA_DV) - acc_ref[1] * _lane_tile(lam / l_ref[1], DA_DV)
    o_ref[...] = (_rms(o) * (gd_ref[...] * (1.0 - LAM_INIT))).astype(bf16)


def _attn_prompt(dq, dkb, dvb, lam_params, g_dnorm, g_qn, g_kn, *, nb, seq, tk):
    hw = 2 * DA_DK
    tq = 2 * tk
    qspec = pl.BlockSpec((tq, hw), lambda b, h, i: (b * (seq // tq) + i, h))
    kvspec = pl.BlockSpec((seq, hw), lambda b, h, i: (b, h))
    vec = _const_spec((1, DA_DK))
    chunk_of = jnp.arange(tk) // CHUNK
    dmask = jnp.where(chunk_of[None, :] <= chunk_of[:, None], 0.0, -1e30).astype(f32)
    return pl.pallas_call(
        functools.partial(_attn_prompt_kernel, tk=tk),
        grid=(nb, DA_HEADS, seq // tq),
        in_specs=[qspec, kvspec, kvspec, vec, vec, vec, vec, _const_spec((1, DA_DV)), vec, vec,
                  _const_spec((tk, tk))],
        out_specs=qspec,
        out_shape=jax.ShapeDtypeStruct(dq.shape, bf16),
        scratch_shapes=[pltpu.VMEM((2, tq, LANES), f32), pltpu.VMEM((2, tq, LANES), f32),
                        pltpu.VMEM((2, tq, DA_DV), f32), pltpu.VMEM((2, tq, tk), f32),
                        pltpu.VMEM((2, tq, tk), f32), pltpu.VMEM((4, tq, LANES), f32),
                        pltpu.VMEM((4, tq, tk), bf16)],
        compiler_params=_params(("parallel", "parallel", "arbitrary")),
        name="attn_prompt",
    )(dq, dkb, dvb, *lam_params, g_dnorm, g_qn, g_kn, dmask)


def _attn_decode_kernel(q_ref, kc_ref, vc_ref, kn_ref, vn_ref, lq1_ref, lk1_ref, lq2_ref, lk2_ref,
                        gd_ref, o_ref, *, past):
    lam = _lam(lq1_ref, lk1_ref, lq2_ref, lk2_ref)
    rows_per_token = DA_HEADS * 2
    for h in range(DA_HEADS):
        hs = slice(h * DA_DV, (h + 1) * DA_DV)
        probs = []
        for a in range(2):
            cols = slice(h * DA_DV + a * DA_DK, h * DA_DV + (a + 1) * DA_DK)
            q = q_ref[:, cols]
            kc = kc_ref[pl.ds(h * 2 + a, past, stride=rows_per_token), :].astype(bf16)
            kn = kn_ref[:, cols]
            sc = lax.dot_general(q, kc, NT_DIMS, preferred_element_type=f32)
            sn = lax.dot_general(q, kn, NT_DIMS, preferred_element_type=f32)
            m = jnp.maximum(jnp.max(sc, axis=1, keepdims=True), jnp.max(sn, axis=1, keepdims=True))
            pc = jnp.exp2(sc - m)
            pn = jnp.exp2(sn - m)
            inv = 1.0 / (jnp.sum(pc, axis=1, keepdims=True) + jnp.sum(pn, axis=1, keepdims=True))
            probs.append((pc * inv, pn * inv))
        ac = probs[0][0] - lam * probs[1][0]
        an = probs[0][1] - lam * probs[1][1]
        o = jnp.dot(ac.astype(bf16), vc_ref[:, hs].astype(bf16), preferred_element_type=f32)
        o = o + jnp.dot(an.astype(bf16), vn_ref[:, hs], preferred_element_type=f32)
        o_ref[:, hs] = (_rms(o) * gd_ref[...] * (1.0 - LAM_INIT)).astype(bf16)


def _attn_decode(dq_v, dkb_v, dvb_v, cache_k_rows, cache_v2d, lam_params, g_dnorm, *, nb, t, past):
    new = pl.BlockSpec((t, IN_COLS), lambda b: (0, b))
    old_k = pl.BlockSpec((past * DA_HEADS * 2, DA_DK), lambda b: (b, 0))
    old_v = pl.BlockSpec((past, IN_COLS), lambda b: (b, 0))
    vec = _const_spec((1, DA_DK))
    return pl.pallas_call(
        functools.partial(_attn_decode_kernel, past=past),
        grid=(nb,),
        in_specs=[new, old_k, old_v, new, new, vec, vec, vec, vec, _const_spec((1, DA_DV))],
        out_specs=new,
        out_shape=jax.ShapeDtypeStruct(dq_v.shape, bf16),
        compiler_params=_params(("parallel",)),
        name="attn_decode",
    )(dq_v, cache_k_rows, cache_v2d, dkb_v, dvb_v, *lam_params, g_dnorm)


def _merge_kernel(x_ref, oa_ref, ob_ref, ga_ref, gb_ref, wr_ref, wd_ref, wo_ref, y_ref):
    a = jnp.dot(oa_ref[...], wr_ref[...], preferred_element_type=f32)
    b = jnp.dot(ob_ref[...], wd_ref[...], preferred_element_type=f32)
    m = ga_ref[...].astype(f32) * a + gb_ref[...].astype(f32) * b
    y_ref[...] = x_ref[...] + jnp.dot(m.astype(bf16), wo_ref[...], preferred_element_type=f32)


def _merge(x2d, oa, ob, ga, gb, w_ret_br, w_da_br, w_out, *, tm):
    n = x2d.shape[0]
    row = pl.BlockSpec((tm, D_MODEL), lambda i: (i, 0))
    w = _const_spec((D_MODEL, D_MODEL))
    return pl.pallas_call(
        _merge_kernel,
        grid=(n // tm,),
        in_specs=[row] * 5 + [w] * 3,
        out_specs=row,
        out_shape=jax.ShapeDtypeStruct(x2d.shape, f32),
        compiler_params=_params(("parallel",)),
        name="merge",
    )(x2d, oa, ob, ga, gb, w_ret_br, w_da_br, w_out)


FF_CHUNK = 256


U_CHUNKS = 2 * D_FF // LANES


def _ffn_kernel(x_ref, ple_ref, cst_ref, gffn_ref, wup_ref, cw_ref, cb_ref, wdn_ref, gple_ref,
                wpg_ref, wple_ref, y_ref, cnew_ref, uext_ref, act_ref, *, tm, shift, halo, chunk_rows):
    first = pl.program_id(1) == 0
    x1 = x_ref[...]
    xn = (_rms(x1) * gffn_ref[...]).astype(bf16)
    state_rows = cst_ref.shape[1]
    lane_chunks = FF_CHUNK // LANES

    def chunk_slab(t0, c, rows):
        return pl.ds((halo + t0) * U_CHUNKS + c, rows, stride=U_CHUNKS)

    @pl.when(first)
    def _():
        if chunk_rows:
            for c in range(U_CHUNKS):
                uext_ref[chunk_slab(-halo, c, halo), :] = cst_ref[0, state_rows - halo:, c * LANES:(c + 1) * LANES]
        else:
            uext_ref[0:halo, :] = cst_ref[0]

    def up(col):
        ucol = jnp.dot(xn, wup_ref[:, col:col + FF_CHUNK], preferred_element_type=f32)
        if chunk_rows:
            for k in range(lane_chunks):
                uext_ref[chunk_slab(0, col // LANES + k, tm), :] = ucol[:, k * LANES:(k + 1) * LANES]
            cnew_ref[0, 0:state_rows - halo, col:col + FF_CHUNK] = jnp.zeros((state_rows - halo, FF_CHUNK), f32)
            cnew_ref[0, state_rows - halo:, col:col + FF_CHUNK] = ucol[tm - halo:, :]
        else:
            uext_ref[halo:halo + tm, col:col + FF_CHUNK] = ucol

    def tap(back, col):
        if chunk_rows:
            return jnp.concatenate([uext_ref[chunk_slab(-back * shift, col // LANES + k, tm), :]
                                    for k in range(lane_chunks)], axis=1)
        return uext_ref[halo - back * shift:halo - back * shift + tm, col:col + FF_CHUNK]

    def conv(col):
        sl = slice(col, col + FF_CHUNK)
        acc = cw_ref[0:1, sl] * tap(2, col)
        acc = acc + cw_ref[1:2, sl] * tap(1, col)
        acc = acc + cw_ref[2:3, sl] * tap(0, col)
        return acc + cb_ref[:, sl]

    for col in range(0, D_FF, FF_CHUNK):
        up(col)
        up(D_FF + col)
        act_ref[:, col:col + FF_CHUNK] = (_silu(conv(col)) * conv(D_FF + col)).astype(bf16)

    x2 = x1 + jnp.dot(act_ref[...], wdn_ref[...], preferred_element_type=f32)
    if chunk_rows:
        keep = halo * U_CHUNKS
        uext_ref[0:keep, :] = uext_ref[tm * U_CHUNKS:tm * U_CHUNKS + keep, :]
    else:
        tail = uext_ref[tm:tm + halo, :]
        cnew_ref[0] = tail
        uext_ref[0:halo, :] = tail

    gate = jnp.dot((_rms(x2) * gple_ref[...]).astype(bf16), wpg_ref[...], preferred_element_type=f32)
    emb = jnp.dot(ple_ref[...].astype(bf16), wple_ref[...], preferred_element_type=f32)
    y_ref[...] = x2 + _sigmoid(gate) * emb


def _ffn(x1, ple2d, cstate, g_ffn, w_up, conv_w, conv_b, w_down, g_ple, w_ple_gate, w_ple, *,
         tm, shift):
    n = x1.shape[0]
    nseg, crows, _ = cstate.shape
    chunk_rows = shift % SUBLANES != 0
    halo = (CONV_W - 1) * shift
    tiles = n // (nseg * tm)
    ple_dim = ple2d.shape[1]
    resident = _resident
    row = lambda width: pl.BlockSpec((tm, width), lambda s, t: (s * tiles + t, 0))
    seg = pl.BlockSpec((1, crows, 2 * D_FF), lambda s, t: (s, 0, 0))
    uext = ((halo + tm) * U_CHUNKS, LANES) if chunk_rows else (halo + tm, 2 * D_FF)
    return pl.pallas_call(
        functools.partial(_ffn_kernel, tm=tm, shift=shift, halo=halo, chunk_rows=chunk_rows),
        grid=(nseg, tiles),
        in_specs=[row(D_MODEL), row(ple_dim), seg, resident((1, D_MODEL)),
                  resident((D_MODEL, 2 * D_FF)), resident((CONV_W, 2 * D_FF)),
                  resident((1, 2 * D_FF)), resident((D_FF, D_MODEL)), resident((1, D_MODEL)),
                  resident((D_MODEL, D_MODEL)), resident((ple_dim, D_MODEL))],
        out_specs=[row(D_MODEL), seg],
        out_shape=[jax.ShapeDtypeStruct(x1.shape, f32), jax.ShapeDtypeStruct(cstate.shape, f32)],
        scratch_shapes=[pltpu.VMEM(uext, f32), pltpu.VMEM((tm, D_FF), bf16)],
        compiler_params=_params(("parallel", "arbitrary")),
        name="ffn",
    )(x1, ple2d, cstate, g_ffn, w_up, conv_w, conv_b, w_down, g_ple, w_ple_gate, w_ple)


def _rope_freqs():
    return ROPE_THETA ** (-jnp.arange(0, RET_DK, 2, dtype=f32) / RET_DK)


def _rope_finish(cos_half, sin_half):
    sign = jnp.where(jnp.arange(RET_DK) < RET_DK // 2, -1.0, 1.0).astype(f32)
    return (jnp.concatenate([cos_half, cos_half], axis=-1),
            jnp.concatenate([sin_half, sin_half], axis=-1) * sign)


def _rope_tables(pos):
    ang = pos.astype(f32)[:, None] * _rope_freqs()[None, :]
    return _rope_finish(jnp.cos(ang), jnp.sin(ang))


ROPE_SPLIT = 64


def _rope_tables_range(seq):
    inv = _rope_freqs()[None, :]
    hi = (ROPE_SPLIT * jnp.arange(seq // ROPE_SPLIT, dtype=f32))[:, None] * inv
    lo = jnp.arange(ROPE_SPLIT, dtype=f32)[:, None] * inv
    ch, sh = jnp.cos(hi)[:, None, :], jnp.sin(hi)[:, None, :]
    cl, sl = jnp.cos(lo)[None, :, :], jnp.sin(lo)[None, :, :]
    cos =(ch * cl - sh * sl).reshape(seq, RET_DK // 2)
    sin = (sh * cl + ch * sl).reshape(seq, RET_DK // 2)
    return _rope_finish(cos, sin)


def _layer_prompt(x, ple, w, *, tm_in=512, tm_merge=1024, tm_ffn=512, ret_chunk=256, attn_tk=512):
    nb, seq, _ = x.shape
    n = nb * seq
    tm_in, tm_merge, tm_ffn = min(tm_in, seq), min(tm_merge, seq), min(tm_ffn, seq)
    ret_chunk, attn_tk = min(ret_chunk, seq), min(attn_tk, seq // 2)
    x2d = x.reshape(n, D_MODEL)
    cos, sin = _rope_tables_range(seq)
    rqk, rv, rg, dq, dk, dkb, dv, dvb, ga, gb = _in_proj(
        x2d, w["g_mix"], w["w_in"], cos, sin, w["g_qn"], w["g_kn"], tm=tm_in)

    n_chunks = seq // ret_chunk
    s0 = jnp.zeros((nb, RET_HEADS, RET_DK, RET_DV), f32)
    per_stream = lambda a: a.reshape(nb, seq, IN_COLS)
    oa, s_new = _retention(per_stream(rqk), per_stream(rv), per_stream(rg), s0, chunk=ret_chunk,
                           n_chunks=n_chunks, stream_major=True)
    oa = oa.reshape(n, IN_COLS)
    ob = _attn_prompt(dq, dkb, dvb, w["lam"], w["g_dnorm"], w["g_qn"], w["g_kn"],
                      nb=nb, seq=seq, tk=attn_tk)
    x1 = _merge(x2d, oa, ob, ga, gb, w["w_ret_br"], w["w_da_br"], w["w_out"], tm=tm_merge)

    cstate = jnp.zeros((nb, SUBLANES, 2 * D_FF), f32)
    y, ctail = _ffn(x1, ple.reshape(n, -1), cstate, w["g_ffn"], w["w_up"], w["conv_w"], w["conv_b"],
                    w["w_down"], w["g_ple"], w["w_ple_gate"], w["w_ple"], tm=tm_ffn, shift=1)
    return (y.reshape(nb, seq, D_MODEL),
            dk.reshape(nb, seq, DA_HEADS, 2, DA_DK),
            jnp.swapaxes(dv.reshape(nb, seq, 2, DA_HEADS, LANES), 2, 3).reshape(nb, seq, DA_HEADS, DA_DV),
            s_new, ctail[:, SUBLANES - (CONV_W - 1):])


def _layer_decode(x, ple, cache_k, cache_v, state_ret, state_conv, w, *, past_len):
    nb, t, _ = x.shape
    n = nb * t
    past = cache_k.shape[1]
    tm = n

    def to_tm(a):
        return jnp.swapaxes(a, 0, 1).reshape(n, a.shape[-1])

    def from_tm(a):
        return jnp.swapaxes(a.reshape(t, nb, a.shape[-1]), 0, 1)

    cos, sin = _rope_tables(past_len + jnp.arange(n) // nb)
    rqk, rv, rg, dq, dk, dkb, dv, dvb, ga, gb = _in_proj(
        to_tm(x), w["g_mix"], w["w_in"], cos, sin, w["g_qn"], w["g_kn"], tm=tm)

    view = lambda a: a.reshape(t, nb * IN_COLS)
    oa, s_new = _retention(view(rqk), view(rv), view(rg), state_ret, chunk=t, n_chunks=1,
                           stream_major=False)
    ob = _attn_decode(view(dq), view(dkb), view(dvb),
                      cache_k.reshape(nb * past * DA_HEADS * 2, DA_DK),
                      cache_v.reshape(nb * past, DA_HEADS * DA_DV),
                      w["lam"], w["g_dnorm"], nb=nb, t=t, past=past)
    x1 = _merge(to_tm(x), oa.reshape(n, IN_COLS), ob.reshape(n, IN_COLS), ga, gb,
                w["w_ret_br"], w["w_da_br"], w["w_out"], tm=tm)

    halo = (CONV_W - 1) * nb
    cstate = jnp.swapaxes(state_conv, 0, 1).reshape(1, halo, 2 * D_FF)
    y, ctail = _ffn(x1, to_tm(ple), cstate, w["g_ffn"], w["w_up"], w["conv_w"], w["conv_b"],
                    w["w_down"], w["g_ple"], w["w_ple_gate"], w["w_ple"], tm=tm, shift=nb)
    return (from_tm(y),
            jnp.swapaxes(dk.reshape(t, nb, DA_HEADS, 2, DA_DK), 0, 1),
            jnp.transpose(dv.reshape(t, nb, 2, DA_HEADS, LANES), (1, 0, 3, 2, 4)).reshape(
                nb, t, DA_HEADS, DA_DV),
            s_new, jnp.swapaxes(ctail.reshape(CONV_W - 1, nb, 2 * D_FF), 0, 1))


def _layer_weights(i, g_mix, w_in, g_qn, g_kn, lam_q1, lam_k1, lam_q2, lam_k2, g_dnorm, w_ret_br,
                   w_da_br, w_out, g_ffn, w_up, conv_w, conv_b, w_down, g_ple, w_ple_gate, w_ple):
    vec = lambda a: a[i].reshape(1, -1)
    mats = {k: v[i].astype(bf16) for k, v in dict(
        w_in=w_in, w_ret_br=w_ret_br, w_da_br=w_da_br, w_out=w_out, w_up=w_up, w_down=w_down,
        w_ple_gate=w_ple_gate, w_ple=w_ple).items()}
    return dict(
        g_mix=vec(g_mix), g_qn=vec(g_qn), g_kn=vec(g_kn),
        lam=(vec(lam_q1), vec(lam_k1), vec(lam_q2), vec(lam_k2)), g_dnorm=vec(g_dnorm),
        g_ffn=vec(g_ffn), conv_w=conv_w[i], conv_b=vec(conv_b), g_ple=vec(g_ple), **mats)


def kernel(x_prompt, x_sample, cache_k, cache_v, state_ret, state_conv, p_prompt, p_sample, g_mix, w_in, g_qn, g_kn, lam_q1, lam_k1, lam_q2, lam_k2, g_dnorm, w_ret_br, w_da_br, w_out, g_ffn, w_up, conv_w, conv_b, w_down, g_ple, w_ple_gate, w_ple):
    depth = w_in.shape[0]
    yp, ys = x_prompt, x_sample
    outs = [[] for _ in range(8)]
    for i in range(depth):
        w = _layer_weights(i, g_mix, w_in, g_qn, g_kn, lam_q1, lam_k1, lam_q2, lam_k2, g_dnorm,
                           w_ret_br, w_da_br, w_out, g_ffn, w_up, conv_w, conv_b, w_down, g_ple,
                           w_ple_gate, w_ple)
        yp, k1, v1, s1, c1 = _layer_prompt(yp, p_prompt[i], w)
        ys, k2, v2, s2, c2 = _layer_decode(ys, p_sample[i], cache_k[i], cache_v[i], state_ret[i],
                                           state_conv[i], w, past_len=PAST_LEN)
        for lst, val in zip(outs, (k1, k2, v1, v2, s1, s2, c1, c2)):
            lst.append(val)
    return (yp, ys) + tuple(jnp.stack(o) for o in outs)
```

```python
import functools
import math

import jax
import jax.numpy as jnp
import numpy as np
from jax import lax
from jax.experimental import pallas as pl
from jax.experimental.pallas import tpu as pltpu

D_MODEL = 1024
CHUNK = 64
RET_HEADS = 4
RET_DK = 128
RET_DV = 256
DA_HEADS = 4
DA_DK = 128
DA_DV = 256
D_FF = 2816
CONV_W = 3
ROPE_THETA = 10000.0
EPS = 1e-6
PAST_LEN = 1024
LAM_INIT = 0.8 - 0.6 * math.exp(-0.3 * 0)
LOG2E = math.log2(math.e)

LANES = 128
SUBLANES = 8
IN_COLS = 1024
N_IN_BLOCKS = 8
VMEM_LIMIT = 56 * 1024 * 1024

f32 = jnp.float32
bf16 = jnp.bfloat16

NT_DIMS = (((1,), (1,)), ((), ()))
TN_DIMS = (((0,), (0,)), ((), ()))


def _rms(x):
    return x * lax.rsqrt(jnp.mean(x * x, axis=-1, keepdims=True) + EPS)


def _rope(x, cos, sin_signed):
    return x * cos + pltpu.roll(x, RET_DK // 2, 1) * sin_signed


def _sigmoid(x):
    return 0.5 * jnp.tanh(0.5 * x) + 0.5


def _silu(x):
    h = 0.5 * x
    return h * jnp.tanh(h) + h


def _params(sem):
    return pltpu.CompilerParams(dimension_semantics=sem, vmem_limit_bytes=VMEM_LIMIT)


def _const_spec(shape):
    nd = len(shape)
    return pl.BlockSpec(shape, lambda *_: (0,) * nd)


def _in_proj_kernel(x_ref, gmix_ref, w_ref, cos_ref, sin_ref, gq_ref, gk_ref,
                    rqk_ref, rv_ref, rg_ref, dq_ref, dk_ref, dkb_ref, dv_ref, dvb_ref,
                    ga_ref, gb_ref):
    half = x_ref.shape[0] // 2
    h_top = (_rms(x_ref[0:half, :]) * gmix_ref[...]).astype(bf16)
    h_bot = (_rms(x_ref[half:, :]) * gmix_ref[...]).astype(bf16)
    h = jnp.concatenate([h_top, h_bot], axis=0)
    cos = cos_ref[...]
    sin = sin_ref[...]
    n_heads = IN_COLS // LANES

    def block(j):
        return jnp.dot(h, w_ref[:, j * IN_COLS:(j + 1) * IN_COLS], preferred_element_type=f32)

    def heads(z):
        return [(slice(g * LANES, (g + 1) * LANES), z[:, g * LANES:(g + 1) * LANES])
                for g in range(n_heads)]

    tm = x_ref.shape[0]
    w_dk = w_ref[:, 4 * IN_COLS:5 * IN_COLS]
    z_dk = jnp.concatenate([jnp.dot(h_top, w_dk, preferred_element_type=f32),
                            jnp.dot(h_bot, w_dk, preferred_element_type=f32)], axis=0)
    for g, (sl, zg) in enumerate(heads(z_dk)):
        r = _rope(_rms(zg) * gk_ref[...], cos, sin)
        dk_ref[pl.ds(g, tm, stride=n_heads), :] = r
        dkb_ref[:, sl] = r.astype(bf16)

    for sl, zg in heads(block(3)):
        r = _rope(_rms(zg) * gq_ref[...], cos, sin) * (DA_DK ** -0.5 * LOG2E)
        dq_ref[:, sl] = r.astype(bf16)

    for g, (sl, zg) in enumerate(heads(block(0))):
        r = _rope(zg, cos, sin)
        if g >= RET_HEADS:
            r = r * (RET_DK ** -0.5)
        rqk_ref[:, sl] = r.astype(bf16)

    z = block(5)
    for g, (sl, zg) in enumerate(heads(z)):
        head, half = divmod(g, 2)
        dv_ref[pl.ds(half * DA_HEADS + head, tm, stride=n_heads), :] = zg
    dvb_ref[...] = z.astype(bf16)
    z = block(2)
    rg_ref[...] = _silu(z).astype(bf16)
    ga_ref[...] = _sigmoid(block(6)).astype(bf16)
    gb_ref[...] = _sigmoid(block(7)).astype(bf16)
    rv_ref[...] = block(1).astype(bf16)


def _resident(shape):
    nd = len(shape)
    return pl.BlockSpec(shape, lambda *_: (0,) * nd, pipeline_mode=pl.Buffered(1))


def _in_proj(x2d, g_mix, w_in, cos, sin, g_qn, g_kn, *, tm):
    n = x2d.shape[0]
    pos_blocks = cos.shape[0] // tm
    row = pl.BlockSpec((tm, IN_COLS), lambda i: (i, 0))
    tab = pl.BlockSpec((tm, LANES), lambda i: (i % pos_blocks, 0))
    bf = jax.ShapeDtypeStruct((n, IN_COLS), bf16)
    chunks = IN_COLS // LANES
    fl = jax.ShapeDtypeStruct((n * chunks, LANES), f32)
    lin = pl.BlockSpec((tm * chunks, LANES), lambda i: (i, 0))
    return pl.pallas_call(
        _in_proj_kernel,
        grid=(n // tm,),
        in_specs=[row, _resident((1, D_MODEL)), _resident((D_MODEL, N_IN_BLOCKS * IN_COLS)),
                  tab, tab, _resident((1, LANES)), _resident((1, LANES))],
        out_specs=[row, row, row, row, lin, row, lin, row, row, row],
        out_shape=[bf, bf, bf, bf, fl, bf, fl, bf, bf, bf],
        compiler_params=_params(("parallel",)),
        name="in_proj",
    )(x2d, g_mix, w_in, cos, sin, g_qn, g_kn)


def _ret_log_gamma():
    return np.log(1.0 - np.exp2(-5.0 - np.arange(RET_HEADS, dtype=np.float64)))


def _ret_tables(c):
    log_g = jnp.asarray(_ret_log_gamma(), f32)[:, None, None]
    idx = jnp.arange(c, dtype=f32)
    diff = idx[:, None] - idx[None, :]
    decay = jnp.where(diff >= 0, jnp.exp(log_g * jnp.maximum(diff, 0.0)), 0.0)
    qd = jnp.broadcast_to(jnp.exp(log_g * (idx + 1.0)[None, :, None]), (RET_HEADS, c, RET_DV))
    kd = jnp.broadcast_to(jnp.exp(log_g * (c - 1.0 - idx)[None, :, None]), (RET_HEADS, c, RET_DK))
    return decay, qd, kd


RET_STREAMS = 4


def _retention_kernel(qk_ref, v_ref, g_ref, dec_ref, qd_ref, kd_ref, s0_ref, o_ref, s_ref, *,
                      chunk, stream_major):
    @pl.when(pl.program_id(1) == 0)
    def _():
        s_ref[...] = s0_ref[...]

    def cols(ref, s, lo, width):
        if stream_major:
            return ref[s, :, lo:lo + width]
        return ref[:, s * IN_COLS + lo:s * IN_COLS + lo + width]

    log_g = _ret_log_gamma()
    for s in range(s_ref.shape[0]):
        for h in range(RET_HEADS):
            q = cols(qk_ref, s, h * RET_DK, RET_DK)
            k = cols(qk_ref, s, (RET_HEADS + h) * RET_DK, RET_DK)
            v = cols(v_ref, s, h * RET_DV, RET_DV)
            state = s_ref[s, h]
            scores = lax.dot_general(q, k, NT_DIMS, preferred_element_type=f32) * dec_ref[h]
            intra = jnp.dot(scores.astype(bf16), v, preferred_element_type=f32)
            inter = jnp.dot(q, state.astype(bf16), preferred_element_type=f32) * qd_ref[h]
            kw = (k.astype(f32) * kd_ref[h]).astype(bf16)
            s_ref[s, h] = float(np.exp(chunk * log_g[h])) * state + lax.dot_general(
                kw, v, TN_DIMS, preferred_element_type=f32)
            o = _rms(intra + inter)
            gate = cols(g_ref, s, h * RET_DV, RET_DV).astype(f32)
            out = (o * gate).astype(bf16)
            if stream_major:
                o_ref[s, :, h * RET_DV:(h + 1) * RET_DV] = out
            else:
                o_ref[:, s * IN_COLS + h * RET_DV:s * IN_COLS + (h + 1) * RET_DV] = out


def _retention(rqk, rv, rg, s0, *, chunk, n_chunks, stream_major):
    nb = s0.shape[0]
    streams = math.gcd(nb, RET_STREAMS)
    decay, qd, kd = _ret_tables(chunk)
    if stream_major:
        blk = pl.BlockSpec((streams, chunk, IN_COLS), lambda b, c: (b, c, 0))
    else:
        blk = pl.BlockSpec((chunk, streams * IN_COLS), lambda b, c: (0, b))
    st = pl.BlockSpec((streams, RET_HEADS, RET_DK, RET_DV), lambda b, c: (b, 0, 0, 0))
    return pl.pallas_call(
        functools.partial(_retention_kernel, chunk=chunk, stream_major=stream_major),
        grid=(nb // streams, n_chunks),
        in_specs=[blk, blk, blk, _const_spec(decay.shape), _const_spec(qd.shape),
                  _const_spec(kd.shape), st],
        out_specs=[blk, st],
        out_shape=[jax.ShapeDtypeStruct(rqk.shape, bf16), jax.ShapeDtypeStruct(s0.shape, f32)],
        compiler_params=_params(("parallel", "arbitrary")),
        name="retention",
    )(rqk, rv, rg, decay, qd, kd, s0)


def _lam(lq1_ref, lk1_ref, lq2_ref, lk2_ref):
    e1 = jnp.exp(jnp.sum(lq1_ref[...] * lk1_ref[...], axis=-1, keepdims=True))
    e2 = jnp.exp(jnp.sum(lq2_ref[...] * lk2_ref[...], axis=-1, keepdims=True))
    return e1 - e2 + LAM_INIT


ATTN_SUB = 64
FAST_SCORE_MAX = 30.0
SCORE_BOUND_PER_GAIN = 1.02 * LOG2E * DA_DK ** 0.5


def _lane_tile(x, width):
    return jnp.concatenate([x] * (width // LANES), axis=1)


def _attn_prompt_kernel(q_ref, k_ref, v_ref, lq1_ref, lk1_ref, lq2_ref, lk2_ref, gd_ref, gq_ref, gk_ref,
                        dmask_ref, o_ref, m_ref, l_ref, acc_ref, sa_ref, sb_ref, alpha_ref, p_ref, *, tk):
    tq = 2 * tk
    qi = pl.program_id(2)

    gq_max = jnp.max(jnp.abs(gq_ref[...]), axis=1, keepdims=True)
    gk_max = jnp.max(jnp.abs(gk_ref[...]), axis=1, keepdims=True)
    fast = (gq_max * gk_max)[0, 0] * SCORE_BOUND_PER_GAIN <= FAST_SCORE_MAX

    def probs(kb, slot, diag=None, r0=0, first=False):
        start = pl.multiple_of(kb * tk, tk)
        for a in range(2):
            buf = 2 * slot + a
            q = q_ref[r0:, a * DA_DK:(a + 1) * DA_DK]
            k = k_ref[pl.ds(start, tk), a * DA_DK:(a + 1) * DA_DK]
            s = lax.dot_general(q, k, NT_DIMS, preferred_element_type=f32)
            if diag == 0:
                s = jnp.concatenate([s[:tk] + dmask_ref[...], s[tk:]], axis=0)
            elif diag == 1:
                s = s + dmask_ref[...]
            p = jnp.exp2(s)
            part = p[:, 0:LANES]
            for c in range(1, tk // LANES):
                part = part + p[:, c * LANES:(c + 1) * LANES]
            l_ref[a, r0:, :] = part if first else l_ref[a, r0:, :] + part
            p_ref[buf, r0:, :] = p.astype(bf16)

    def pv_add(kb, slot, r0=0):
        start = pl.multiple_of(kb * tk, tk)
        v = v_ref[pl.ds(start, tk), :]
        for a in range(2):
            acc_ref[a, r0:, :] = acc_ref[a, r0:, :] + jnp.dot(
                p_ref[2 * slot + a, r0:, :], v, preferred_element_type=f32)

    def fast_path():
        def pair(i, last_diag):
            probs(2 * i + 1, 1)
            pv_add(2 * i, 0)
            probs(2 * i + 2, 0, diag=last_diag)
            pv_add(2 * i + 1, 1)

        @pl.when(qi == 0)
        def _():
            probs(0, 0, diag=0, first=True)
            acc_ref[...] = jnp.zeros(acc_ref.shape, f32)

        @pl.when(qi > 0)
        def _():
            probs(0, 0, first=True)
            acc_ref[...] = jnp.zeros(acc_ref.shape, f32)

            def body(i, carry):
                pair(2 * i, None)
                pair(2 * i + 1, None)
                return carry

            n_pairs = qi - 1
            lax.fori_loop(0, lax.shift_right_logical(n_pairs, 1), body, 0)

            @pl.when(lax.bitwise_and(n_pairs, 1) == 1)
            def _():
                pair(n_pairs - 1, None)

            pair(qi - 1, 0)

        pv_add(2 * qi, 0)
        probs(2 * qi + 1, 1, diag=1, r0=tk)
        pv_add(2 * qi + 1, 1, r0=tk)
        for a in range(2):
            l_ref[a] = jnp.broadcast_to(jnp.sum(l_ref[a], axis=1, keepdims=True), l_ref.shape[1:])


    def scores(kb, s_ref, r0=0):
        start = pl.multiple_of(kb * tk, tk)
        for a in range(2):
            q = q_ref[r0:, a * DA_DK:(a + 1) * DA_DK]
            k = k_ref[pl.ds(start, tk), a * DA_DK:(a + 1) * DA_DK]
            s_ref[a, r0:, :] = lax.dot_general(q, k, NT_DIMS, preferred_element_type=f32)

    def softmax(slot, diag=None, r0=0):
        s_ref = (sa_ref, sb_ref)[slot]
        for a in range(2):
            buf = 2 * slot + a
            for r in range(r0 // ATTN_SUB, tq // ATTN_SUB):
                rows = slice(r * ATTN_SUB, (r + 1) * ATTN_SUB)
                s = s_ref[a, rows, :]
                if diag is not None and (diag * tk + tk - 1) // CHUNK > (r * ATTN_SUB) // CHUNK:
                    row = (lax.broadcasted_iota(jnp.int32, s.shape, 0) + r * ATTN_SUB) // CHUNK
                    col = (lax.broadcasted_iota(jnp.int32, s.shape, 1) + diag * tk) // CHUNK
                    s = jnp.where(col <= row, s, -1e30)
                m_prev = m_ref[a, rows, :]
                m_new = jnp.maximum(m_prev, jnp.max(s, axis=1, keepdims=True))
                alpha = jnp.exp2(m_prev - m_new)
                p = jnp.exp2(s - _lane_tile(m_new, tk))
                l_ref[a, rows, :] = alpha * l_ref[a, rows, :] + jnp.sum(p, axis=1, keepdims=True)
                m_ref[a, rows, :] = m_new
                alpha_ref[buf, rows, :] = alpha
                p_ref[buf, rows, :] = p.astype(bf16)

    def pv(kb, slot, r0=0):
        start = pl.multiple_of(kb * tk, tk)
        v = v_ref[pl.ds(start, tk), :]
        for a in range(2):
            buf = 2 * slot + a
            acc_ref[a, r0:, :] = acc_ref[a, r0:, :] * _lane_tile(alpha_ref[buf, r0:, :], DA_DV) + jnp.dot(
                p_ref[buf, r0:, :], v, preferred_element_type=f32)

    def safe_path():
        m_ref[...] = jnp.full(m_ref.shape, -1e30, f32)
        l_ref[...] = jnp.zeros(l_ref.shape, f32)
        acc_ref[...] = jnp.zeros(acc_ref.shape, f32)
        alpha_ref[2:4] = jnp.ones((2,) + alpha_ref.shape[1:], f32)
        p_ref[2:4] = jnp.zeros((2,) + p_ref.shape[1:], bf16)
        scores(0, sa_ref)

        def body(i, carry):
            kb = 2 * i
            pv(jnp.maximum(kb - 1, 0), 1)
            scores(kb + 1, sb_ref)
            softmax(0)
            pv(kb, 0)
            scores(kb + 2, sa_ref)
            softmax(1)
            return carry

        lax.fori_loop(0, qi, body, 0)
        pv(jnp.maximum(2 * qi - 1, 0), 1)
        scores(2 * qi + 1, sb_ref, r0=tk)
        softmax(0, diag=0)
        pv(2 * qi, 0)
        softmax(1, diag=1, r0=tk)
        pv(2 * qi + 1, 1, r0=tk)

    pl.when(fast)(fast_path)
    pl.when(jnp.logical_not(fast))(safe_path)

    lam = _lam(lq1_ref, lk1_ref, lq2_ref, lk2_ref)
    o = acc_ref[0] * _lane_tile(1.0 / l_ref[0], DA_DV) - acc_ref[1] * _lane_tile(lam / l_ref[1], DA_DV)
    o_ref[...] = (_rms(o) * (gd_ref[...] * (1.0 - LAM_INIT))).astype(bf16)


def _attn_prompt(dq, dkb, dvb, lam_params, g_dnorm, g_qn, g_kn, *, nb, seq, tk):
    hw = 2 * DA_DK
    tq = 2 * tk
    qspec = pl.BlockSpec((tq, hw), lambda b, h, i: (b * (seq // tq) + i, h))
    kvspec = pl.BlockSpec((seq, hw), lambda b, h, i: (b, h))
    vec = _const_spec((1, DA_DK))
    chunk_of = jnp.arange(tk) // CHUNK
    dmask = jnp.where(chunk_of[None, :] <= chunk_of[:, None], 0.0, -1e30).astype(f32)
    return pl.pallas_call(
        functools.partial(_attn_prompt_kernel, tk=tk),
        grid=(nb, DA_HEADS, seq // tq),
        in_specs=[qspec, kvspec, kvspec, vec, vec, vec, vec, _const_spec((1, DA_DV)), vec, vec,
                  _const_spec((tk, tk))],
        out_specs=qspec,
        out_shape=jax.ShapeDtypeStruct(dq.shape, bf16),
        scratch_shapes=[pltpu.VMEM((2, tq, LANES), f32), pltpu.VMEM((2, tq, LANES), f32),
                        pltpu.VMEM((2, tq, DA_DV), f32), pltpu.VMEM((2, tq, tk), f32),
                        pltpu.VMEM((2, tq, tk), f32), pltpu.VMEM((4, tq, LANES), f32),
                        pltpu.VMEM((4, tq, tk), bf16)],
        compiler_params=_params(("parallel", "parallel", "arbitrary")),
        name="attn_prompt",
    )(dq, dkb, dvb, *lam_params, g_dnorm, g_qn, g_kn, dmask)


def _attn_decode_kernel(q_ref, kc_ref, vc_ref, kn_ref, vn_ref, lq1_ref, lk1_ref, lq2_ref, lk2_ref,
                        gd_ref, o_ref, *, past):
    lam = _lam(lq1_ref, lk1_ref, lq2_ref, lk2_ref)
    rows_per_token = DA_HEADS * 2
    for h in range(DA_HEADS):
        hs = slice(h * DA_DV, (h + 1) * DA_DV)
        probs = []
        for a in range(2):
            cols = slice(h * DA_DV + a * DA_DK, h * DA_DV + (a + 1) * DA_DK)
            q = q_ref[:, cols]
            kc = kc_ref[pl.ds(h * 2 + a, past, stride=rows_per_token), :].astype(bf16)
            kn = kn_ref[:, cols]
            sc = lax.dot_general(q, kc, NT_DIMS, preferred_element_type=f32)
            sn = lax.dot_general(q, kn, NT_DIMS, preferred_element_type=f32)
            m = jnp.maximum(jnp.max(sc, axis=1, keepdims=True), jnp.max(sn, axis=1, keepdims=True))
            pc = jnp.exp2(sc - m)
            pn = jnp.exp2(sn - m)
            inv = 1.0 / (jnp.sum(pc, axis=1, keepdims=True) + jnp.sum(pn, axis=1, keepdims=True))
            probs.append((pc * inv, pn * inv))
        ac = probs[0][0] - lam * probs[1][0]
        an = probs[0][1] - lam * probs[1][1]
        vc = jnp.concatenate([vc_ref[pl.ds(half * DA_HEADS + h, past, stride=rows_per_token), :]
                              for half in range(2)], axis=1).astype(bf16)
        o = jnp.dot(ac.astype(bf16), vc, preferred_element_type=f32)
        o = o + jnp.dot(an.astype(bf16), vn_ref[:, hs], preferred_element_type=f32)
        o_ref[:, hs] = (_rms(o) * gd_ref[...] * (1.0 - LAM_INIT)).astype(bf16)


def _attn_decode(dq_v, dkb_v, dvb_v, cache_k_rows, cache_v_rows, lam_params, g_dnorm, *, nb, t, past):
    new = pl.BlockSpec((t, IN_COLS), lambda b: (0, b))
    old_k = pl.BlockSpec((past * DA_HEADS * 2, DA_DK), lambda b: (b, 0))
    old_v = old_k
    vec = _const_spec((1, DA_DK))
    return pl.pallas_call(
        functools.partial(_attn_decode_kernel, past=past),
        grid=(nb,),
        in_specs=[new, old_k, old_v, new, new, vec, vec, vec, vec, _const_spec((1, DA_DV))],
        out_specs=new,
        out_shape=jax.ShapeDtypeStruct(dq_v.shape, bf16),
        compiler_params=_params(("parallel",)),
        name="attn_decode",
    )(dq_v, cache_k_rows, cache_v_rows, dkb_v, dvb_v, *lam_params, g_dnorm)


def _merge_kernel(x_ref, oa_ref, ob_ref, ga_ref, gb_ref, wr_ref, wd_ref, wo_ref, y_ref):
    a = jnp.dot(oa_ref[...], wr_ref[...], preferred_element_type=f32)
    b = jnp.dot(ob_ref[...], wd_ref[...], preferred_element_type=f32)
    m = ga_ref[...].astype(f32) * a + gb_ref[...].astype(f32) * b
    y_ref[...] = x_ref[...] + jnp.dot(m.astype(bf16), wo_ref[...], preferred_element_type=f32)


def _merge(x2d, oa, ob, ga, gb, w_ret_br, w_da_br, w_out, *, tm):
    n = x2d.shape[0]
    row = pl.BlockSpec((tm, D_MODEL), lambda i: (i, 0))
    w = _const_spec((D_MODEL, D_MODEL))
    return pl.pallas_call(
        _merge_kernel,
        grid=(n // tm,),
        in_specs=[row] * 5 + [w] * 3,
        out_specs=row,
        out_shape=jax.ShapeDtypeStruct(x2d.shape, f32),
        compiler_params=_params(("parallel",)),
        name="merge",
    )(x2d, oa, ob, ga, gb, w_ret_br, w_da_br, w_out)


FF_CHUNK = 256


U_CHUNKS = 2 * D_FF // LANES


def _ffn_kernel(x_ref, ple_ref, cst_ref, gffn_ref, wup_ref, cw_ref, cb_ref, wdn_ref, gple_ref,
                wpg_ref, wple_ref, y_ref, cnew_ref, uext_ref, act_ref, *, tm, shift, halo, chunk_rows):
    first = pl.program_id(1) == 0
    x1 = x_ref[...]
    xn = (_rms(x1) * gffn_ref[...]).astype(bf16)
    state_rows = cst_ref.shape[1]
    lane_chunks = FF_CHUNK // LANES

    def chunk_slab(t0, c, rows):
        return pl.ds((halo + t0) * U_CHUNKS + c, rows, stride=U_CHUNKS)

    @pl.when(first)
    def _():
        if chunk_rows:
            for c in range(U_CHUNKS):
                uext_ref[chunk_slab(-halo, c, halo), :] = cst_ref[0, state_rows - halo:, c * LANES:(c + 1) * LANES]
        else:
            uext_ref[0:halo, :] = cst_ref[0]

    def up(col):
        ucol = jnp.dot(xn, wup_ref[:, col:col + FF_CHUNK], preferred_element_type=f32)
        if chunk_rows:
            for k in range(lane_chunks):
                uext_ref[chunk_slab(0, col // LANES + k, tm), :] = ucol[:, k * LANES:(k + 1) * LANES]
            cnew_ref[0, 0:state_rows - halo, col:col + FF_CHUNK] = jnp.zeros((state_rows - halo, FF_CHUNK), f32)
            cnew_ref[0, state_rows - halo:, col:col + FF_CHUNK] = ucol[tm - halo:, :]
        else:
            uext_ref[halo:halo + tm, col:col + FF_CHUNK] = ucol

    def tap(back, col):
        if chunk_rows:
            return jnp.concatenate([uext_ref[chunk_slab(-back * shift, col // LANES + k, tm), :]
                                    for k in range(lane_chunks)], axis=1)
        return uext_ref[halo - back * shift:halo - back * shift + tm, col:col + FF_CHUNK]

    def conv(col):
        sl = slice(col, col + FF_CHUNK)
        acc = cw_ref[0:1, sl] * tap(2, col)
        acc = acc + cw_ref[1:2, sl] * tap(1, col)
        acc = acc + cw_ref[2:3, sl] * tap(0, col)
        return acc + cb_ref[:, sl]

    for col in range(0, D_FF, FF_CHUNK):
        up(col)
        up(D_FF + col)
        act_ref[:, col:col + FF_CHUNK] = (_silu(conv(col)) * conv(D_FF + col)).astype(bf16)

    x2 = x1 + jnp.dot(act_ref[...], wdn_ref[...], preferred_element_type=f32)
    if chunk_rows:
        keep = halo * U_CHUNKS
        uext_ref[0:keep, :] = uext_ref[tm * U_CHUNKS:tm * U_CHUNKS + keep, :]
    else:
        tail = uext_ref[tm:tm + halo, :]
        cnew_ref[0] = tail
        uext_ref[0:halo, :] = tail

    gate = jnp.dot((_rms(x2) * gple_ref[...]).astype(bf16), wpg_ref[...], preferred_element_type=f32)
    emb = jnp.dot(ple_ref[...].astype(bf16), wple_ref[...], preferred_element_type=f32)
    y_ref[...] = x2 + _sigmoid(gate) * emb


def _ffn(x1, ple2d, cstate, g_ffn, w_up, conv_w, conv_b, w_down, g_ple, w_ple_gate, w_ple, *,
         tm, shift):
    n = x1.shape[0]
    nseg, crows, _ = cstate.shape
    chunk_rows = shift % SUBLANES != 0
    halo = (CONV_W - 1) * shift
    tiles = n // (nseg * tm)
    ple_dim = ple2d.shape[1]
    resident = _resident
    row = lambda width: pl.BlockSpec((tm, width), lambda s, t: (s * tiles + t, 0))
    seg = pl.BlockSpec((1, crows, 2 * D_FF), lambda s, t: (s, 0, 0))
    uext = ((halo + tm) * U_CHUNKS, LANES) if chunk_rows else (halo + tm, 2 * D_FF)
    return pl.pallas_call(
        functools.partial(_ffn_kernel, tm=tm, shift=shift, halo=halo, chunk_rows=chunk_rows),
        grid=(nseg, tiles),
        in_specs=[row(D_MODEL), row(ple_dim), seg, resident((1, D_MODEL)),
                  resident((D_MODEL, 2 * D_FF)), resident((CONV_W, 2 * D_FF)),
                  resident((1, 2 * D_FF)), resident((D_FF, D_MODEL)), resident((1, D_MODEL)),
                  resident((D_MODEL, D_MODEL)), resident((ple_dim, D_MODEL))],
        out_specs=[row(D_MODEL), seg],
        out_shape=[jax.ShapeDtypeStruct(x1.shape, f32), jax.ShapeDtypeStruct(cstate.shape, f32)],
        scratch_shapes=[pltpu.VMEM(uext, f32), pltpu.VMEM((tm, D_FF), bf16)],
        compiler_params=_params(("parallel", "arbitrary")),
        name="ffn",
    )(x1, ple2d, cstate, g_ffn, w_up, conv_w, conv_b, w_down, g_ple, w_ple_gate, w_ple)


def _rope_freqs():
    return ROPE_THETA ** (-jnp.arange(0, RET_DK, 2, dtype=f32) / RET_DK)


def _rope_finish(cos_half, sin_half):
    sign = jnp.where(jnp.arange(RET_DK) < RET_DK // 2, -1.0, 1.0).astype(f32)
    return (jnp.concatenate([cos_half, cos_half], axis=-1),
            jnp.concatenate([sin_half, sin_half], axis=-1) * sign)


def _rope_tables(pos):
    ang = pos.astype(f32)[:, None] * _rope_freqs()[None, :]
    return _rope_finish(jnp.cos(ang), jnp.sin(ang))


ROPE_SPLIT = 64


def _rope_tables_range(seq):
    inv = _rope_freqs()[None, :]
    hi = (ROPE_SPLIT * jnp.arange(seq // ROPE_SPLIT, dtype=f32))[:, None] * inv
    lo = jnp.arange(ROPE_SPLIT, dtype=f32)[:, None] * inv
    ch, sh = jnp.cos(hi)[:, None, :], jnp.sin(hi)[:, None, :]
    cl, sl = jnp.cos(lo)[None, :, :], jnp.sin(lo)[None, :, :]
    cos =(ch * cl - sh * sl).reshape(seq, RET_DK // 2)
    sin = (sh * cl + ch * sl).reshape(seq, RET_DK // 2)
    return _rope_finish(cos, sin)


def _layer_prompt(x, ple, w, *, tm_in=512, tm_merge=1024, tm_ffn=512, ret_chunk=256, attn_tk=512):
    nb, seq, _ = x.shape
    n = nb * seq
    tm_in, tm_merge, tm_ffn = min(tm_in, seq), min(tm_merge, seq), min(tm_ffn, seq)
    ret_chunk, attn_tk = min(ret_chunk, seq), min(attn_tk, seq // 2)
    x2d = x.reshape(n, D_MODEL)
    cos, sin = _rope_tables_range(seq)
    rqk, rv, rg, dq, dk, dkb, dv, dvb, ga, gb = _in_proj(
        x2d, w["g_mix"], w["w_in"], cos, sin, w["g_qn"], w["g_kn"], tm=tm_in)

    n_chunks = seq // ret_chunk
    s0 = jnp.zeros((nb, RET_HEADS, RET_DK, RET_DV), f32)
    per_stream = lambda a: a.reshape(nb, seq, IN_COLS)
    oa, s_new = _retention(per_stream(rqk), per_stream(rv), per_stream(rg), s0, chunk=ret_chunk,
                           n_chunks=n_chunks, stream_major=True)
    oa = oa.reshape(n, IN_COLS)
    ob = _attn_prompt(dq, dkb, dvb, w["lam"], w["g_dnorm"], w["g_qn"], w["g_kn"],
                      nb=nb, seq=seq, tk=attn_tk)
    x1 = _merge(x2d, oa, ob, ga, gb, w["w_ret_br"], w["w_da_br"], w["w_out"], tm=tm_merge)

    cstate = jnp.zeros((nb, SUBLANES, 2 * D_FF), f32)
    y, ctail = _ffn(x1, ple.reshape(n, -1), cstate, w["g_ffn"], w["w_up"], w["conv_w"], w["conv_b"],
                    w["w_down"], w["g_ple"], w["w_ple_gate"], w["w_ple"], tm=tm_ffn, shift=1)
    return (y.reshape(nb, seq, D_MODEL),
            dk.reshape(nb, seq, DA_HEADS, 2, DA_DK),
            jnp.swapaxes(dv.reshape(nb, seq, 2, DA_HEADS, LANES), 2, 3).reshape(nb, seq, DA_HEADS, DA_DV),
            s_new, ctail[:, SUBLANES - (CONV_W - 1):])


def _layer_decode(x, ple, cache_k, cache_v, state_ret, state_conv, w, *, past_len):
    nb, t, _ = x.shape
    n = nb * t
    past = cache_k.shape[1]
    tm = n

    def to_tm(a):
        return jnp.swapaxes(a, 0, 1).reshape(n, a.shape[-1])

    def from_tm(a):
        return jnp.swapaxes(a.reshape(t, nb, a.shape[-1]), 0, 1)

    cos, sin = _rope_tables(past_len + jnp.arange(n) // nb)
    rqk, rv, rg, dq, dk, dkb, dv, dvb, ga, gb = _in_proj(
        to_tm(x), w["g_mix"], w["w_in"], cos, sin, w["g_qn"], w["g_kn"], tm=tm)

    view = lambda a: a.reshape(t, nb * IN_COLS)
    oa, s_new = _retention(view(rqk), view(rv), view(rg), state_ret, chunk=t, n_chunks=1,
                           stream_major=False)
    ob = _attn_decode(view(dq), view(dkb), view(dvb),
                      cache_k.reshape(nb * past * DA_HEADS * 2, DA_DK),
                      jnp.swapaxes(cache_v.reshape(nb, past, DA_HEADS, 2, LANES), 2, 3).reshape(
                          nb * past * DA_HEADS * 2, LANES),
                      w["lam"], w["g_dnorm"], nb=nb, t=t, past=past)
    x1 = _merge(to_tm(x), oa.reshape(n, IN_COLS), ob.reshape(n, IN_COLS), ga, gb,
                w["w_ret_br"], w["w_da_br"], w["w_out"], tm=tm)

    halo = (CONV_W - 1) * nb
    cstate = jnp.swapaxes(state_conv, 0, 1).reshape(1, halo, 2 * D_FF)
    y, ctail = _ffn(x1, to_tm(ple), cstate, w["g_ffn"], w["w_up"], w["conv_w"], w["conv_b"],
                    w["w_down"], w["g_ple"], w["w_ple_gate"], w["w_ple"], tm=tm, shift=nb)
    return (from_tm(y),
            jnp.swapaxes(dk.reshape(t, nb, DA_HEADS, 2, DA_DK), 0, 1),
            jnp.transpose(dv.reshape(t, nb, 2, DA_HEADS, LANES), (1, 0, 3, 2, 4)).reshape(
                nb, t, DA_HEADS, DA_DV),
            s_new, jnp.swapaxes(ctail.reshape(CONV_W - 1, nb, 2 * D_FF), 0, 1))


def _layer_weights(i, g_mix, w_in, g_qn, g_kn, lam_q1, lam_k1, lam_q2, lam_k2, g_dnorm, w_ret_br,
                   w_da_br, w_out, g_ffn, w_up, conv_w, conv_b, w_down, g_ple, w_ple_gate, w_ple):
    vec = lambda a: a[i].reshape(1, -1)
    mats = {k: v[i].astype(bf16) for k, v in dict(
        w_in=w_in, w_ret_br=w_ret_br, w_da_br=w_da_br, w_out=w_out, w_up=w_up, w_down=w_down,
        w_ple_gate=w_ple_gate, w_ple=w_ple).items()}
    return dict(
        g_mix=vec(g_mix), g_qn=vec(g_qn), g_kn=vec(g_kn),
        lam=(vec(lam_q1), vec(lam_k1), vec(lam_q2), vec(lam_k2)), g_dnorm=vec(g_dnorm),
        g_ffn=vec(g_ffn), conv_w=conv_w[i], conv_b=vec(conv_b), g_ple=vec(g_ple), **mats)


def kernel(x_prompt, x_sample, cache_k, cache_v, state_ret, state_conv, p_prompt, p_sample, g_mix, w_in, g_qn, g_kn, lam_q1, lam_k1, lam_q2, lam_k2, g_dnorm, w_ret_br, w_da_br, w_out, g_ffn, w_up, conv_w, conv_b, w_down, g_ple, w_ple_gate, w_ple):
    depth = w_in.shape[0]
    yp, ys = x_prompt, x_sample
    outs = [[] for _ in range(8)]
    for i in range(depth):
        w = _layer_weights(i, g_mix, w_in, g_qn, g_kn, lam_q1, lam_k1, lam_q2, lam_k2, g_dnorm,
                           w_ret_br, w_da_br, w_out, g_ffn, w_up, conv_w, conv_b, w_down, g_ple,
                           w_ple_gate, w_ple)
        yp, k1, v1, s1, c1 = _layer_prompt(yp, p_prompt[i], w)
        ys, k2, v2, s2, c2 = _layer_decode(ys, p_sample[i], cache_k[i], cache_v[i], state_ret[i],
                                           state_conv[i], w, past_len=PAST_LEN)
        for lst, val in zip(outs, (k1, k2, v1, v2, s1, s2, c1, c2)):
            lst.append(val)
    return (yp, ys) + tuple(jnp.stack(o) for o in outs)
```

```python
import functools
import math

import jax
import jax.numpy as jnp
import numpy as np
from jax import lax
from jax.experimental import pallas as pl
from jax.experimental.pallas import tpu as pltpu

D_MODEL = 1024
CHUNK = 64
RET_HEADS = 4
RET_DK = 128
RET_DV = 256
DA_HEADS = 4
DA_DK = 128
DA_DV = 256
D_FF = 2816
CONV_W = 3
ROPE_THETA = 10000.0
EPS = 1e-6
PAST_LEN = 1024
LAM_INIT = 0.8 - 0.6 * math.exp(-0.3 * 0)
LOG2E = math.log2(math.e)

LANES = 128
SUBLANES = 8
IN_COLS = 1024
N_IN_BLOCKS = 8
VMEM_LIMIT = 56 * 1024 * 1024

f32 = jnp.float32
bf16 = jnp.bfloat16

NT_DIMS = (((1,), (1,)), ((), ()))
TN_DIMS = (((0,), (0,)), ((), ()))


def _rms(x):
    return x * lax.rsqrt(jnp.mean(x * x, axis=-1, keepdims=True) + EPS)


def _rope(x, cos, sin_signed):
    return x * cos + pltpu.roll(x, RET_DK // 2, 1) * sin_signed


def _sigmoid(x):
    return 0.5 * jnp.tanh(0.5 * x) + 0.5


def _silu(x):
    h = 0.5 * x
    return h * jnp.tanh(h) + h


def _params(sem):
    return pltpu.CompilerParams(dimension_semantics=sem, vmem_limit_bytes=VMEM_LIMIT)


def _const_spec(shape):
    nd = len(shape)
    return pl.BlockSpec(shape, lambda *_: (0,) * nd)


def _in_proj_kernel(x_ref, gmix_ref, w_ref, cos_ref, sin_ref, gq_ref, gk_ref,
                    rqk_ref, rv_ref, rg_ref, dq_ref, dk_ref, dkb_ref, dv_ref, dvb_ref,
                    ga_ref, gb_ref):
    half = x_ref.shape[0] // 2
    h_top = (_rms(x_ref[0:half, :]) * gmix_ref[...]).astype(bf16)
    h_bot = (_rms(x_ref[half:, :]) * gmix_ref[...]).astype(bf16)
    h = jnp.concatenate([h_top, h_bot], axis=0)
    cos = cos_ref[...]
    sin = sin_ref[...]
    n_heads = IN_COLS // LANES

    def block(j):
        return jnp.dot(h, w_ref[:, j * IN_COLS:(j + 1) * IN_COLS], preferred_element_type=f32)

    def heads(z):
        return [(slice(g * LANES, (g + 1) * LANES), z[:, g * LANES:(g + 1) * LANES])
                for g in range(n_heads)]

    tm = x_ref.shape[0]
    w_dk = w_ref[:, 4 * IN_COLS:5 * IN_COLS]
    z_dk = jnp.concatenate([jnp.dot(h_top, w_dk, preferred_element_type=f32),
                            jnp.dot(h_bot, w_dk, preferred_element_type=f32)], axis=0)
    for g, (sl, zg) in enumerate(heads(z_dk)):
        r = _rope(_rms(zg) * gk_ref[...], cos, sin)
        dk_ref[pl.ds(g, tm, stride=n_heads), :] = r
        dkb_ref[:, sl] = r.astype(bf16)

    for sl, zg in heads(block(3)):
        r = _rope(_rms(zg) * gq_ref[...], cos, sin) * (DA_DK ** -0.5 * LOG2E)
        dq_ref[:, sl] = r.astype(bf16)

    for g, (sl, zg) in enumerate(heads(block(0))):
        r = _rope(zg, cos, sin)
        if g >= RET_HEADS:
            r = r * (RET_DK ** -0.5)
        rqk_ref[:, sl] = r.astype(bf16)

    z = block(5)
    for g, (sl, zg) in enumerate(heads(z)):
        head, half = divmod(g, 2)
        dv_ref[pl.ds(half * DA_HEADS + head, tm, stride=n_heads), :] = zg
    dvb_ref[...] = z.astype(bf16)
    z = block(2)
    rg_ref[...] = _silu(z).astype(bf16)
    ga_ref[...] = _sigmoid(block(6)).astype(bf16)
    gb_ref[...] = _sigmoid(block(7)).astype(bf16)
    rv_ref[...] = block(1).astype(bf16)


def _resident(shape):
    nd = len(shape)
    return pl.BlockSpec(shape, lambda *_: (0,) * nd, pipeline_mode=pl.Buffered(1))


def _in_proj(x2d, g_mix, w_in, cos, sin, g_qn, g_kn, *, tm):
    n = x2d.shape[0]
    pos_blocks = cos.shape[0] // tm
    row = pl.BlockSpec((tm, IN_COLS), lambda i: (i, 0))
    tab = pl.BlockSpec((tm, LANES), lambda i: (i % pos_blocks, 0))
    bf = jax.ShapeDtypeStruct((n, IN_COLS), bf16)
    chunks = IN_COLS // LANES
    fl = jax.ShapeDtypeStruct((n * chunks, LANES), f32)
    lin = pl.BlockSpec((tm * chunks, LANES), lambda i: (i, 0))
    return pl.pallas_call(
        _in_proj_kernel,
        grid=(n // tm,),
        in_specs=[row, _resident((1, D_MODEL)), _resident((D_MODEL, N_IN_BLOCKS * IN_COLS)),
                  tab, tab, _resident((1, LANES)), _resident((1, LANES))],
        out_specs=[row, row, row, row, lin, row, lin, row, row, row],
        out_shape=[bf, bf, bf, bf, fl, bf, fl, bf, bf, bf],
        compiler_params=_params(("parallel",)),
        name="in_proj",
    )(x2d, g_mix, w_in, cos, sin, g_qn, g_kn)


def _ret_log_gamma():
    return np.log(1.0 - np.exp2(-5.0 - np.arange(RET_HEADS, dtype=np.float64)))


def _ret_tables(c):
    log_g = jnp.asarray(_ret_log_gamma(), f32)[:, None, None]
    idx = jnp.arange(c, dtype=f32)
    diff = idx[:, None] - idx[None, :]
    decay = jnp.where(diff >= 0, jnp.exp(log_g * jnp.maximum(diff, 0.0)), 0.0)
    qd = jnp.broadcast_to(jnp.exp(log_g * (idx + 1.0)[None, :, None]), (RET_HEADS, c, RET_DV))
    kd = jnp.broadcast_to(jnp.exp(log_g * (c - 1.0 - idx)[None, :, None]), (RET_HEADS, c, RET_DK))
    return decay, qd, kd


RET_STREAMS = 4


def _retention_kernel(qk_ref, v_ref, g_ref, dec_ref, qd_ref, kd_ref, s0_ref, o_ref, s_ref, *,
                      chunk, stream_major):
    @pl.when(pl.program_id(1) == 0)
    def _():
        s_ref[...] = s0_ref[...]

    def cols(ref, s, lo, width):
        if stream_major:
            return ref[s, :, lo:lo + width]
        return ref[:, s * IN_COLS + lo:s * IN_COLS + lo + width]

    log_g = _ret_log_gamma()
    for s in range(s_ref.shape[0]):
        for h in range(RET_HEADS):
            q = cols(qk_ref, s, h * RET_DK, RET_DK)
            k = cols(qk_ref, s, (RET_HEADS + h) * RET_DK, RET_DK)
            v = cols(v_ref, s, h * RET_DV, RET_DV)
            state = s_ref[s, h]
            scores = lax.dot_general(q, k, NT_DIMS, preferred_element_type=f32) * dec_ref[h]
            intra = jnp.dot(scores.astype(bf16), v, preferred_element_type=f32)
            inter = jnp.dot(q, state.astype(bf16), preferred_element_type=f32) * qd_ref[h]
            kw = (k.astype(f32) * kd_ref[h]).astype(bf16)
            s_ref[s, h] = float(np.exp(chunk * log_g[h])) * state + lax.dot_general(
                kw, v, TN_DIMS, preferred_element_type=f32)
            o = _rms(intra + inter)
            gate = cols(g_ref, s, h * RET_DV, RET_DV).astype(f32)
            out = (o * gate).astype(bf16)
            if stream_major:
                o_ref[s, :, h * RET_DV:(h + 1) * RET_DV] = out
            else:
                o_ref[:, s * IN_COLS + h * RET_DV:s * IN_COLS + (h + 1) * RET_DV] = out


def _retention(rqk, rv, rg, s0, *, chunk, n_chunks, stream_major):
    nb = s0.shape[0]
    streams = math.gcd(nb, RET_STREAMS)
    decay, qd, kd = _ret_tables(chunk)
    if stream_major:
        blk = pl.BlockSpec((streams, chunk, IN_COLS), lambda b, c: (b, c, 0))
    else:
        blk = pl.BlockSpec((chunk, streams * IN_COLS), lambda b, c: (0, b))
    st = pl.BlockSpec((streams, RET_HEADS, RET_DK, RET_DV), lambda b, c: (b, 0, 0, 0))
    return pl.pallas_call(
        functools.partial(_retention_kernel, chunk=chunk, stream_major=stream_major),
        grid=(nb // streams, n_chunks),
        in_specs=[blk, blk, blk, _const_spec(decay.shape), _const_spec(qd.shape),
                  _const_spec(kd.shape), st],
        out_specs=[blk, st],
        out_shape=[jax.ShapeDtypeStruct(rqk.shape, bf16), jax.ShapeDtypeStruct(s0.shape, f32)],
        compiler_params=_params(("parallel", "arbitrary")),
        name="retention",
    )(rqk, rv, rg, decay, qd, kd, s0)


def _lam(lq1_ref, lk1_ref, lq2_ref, lk2_ref):
    e1 = jnp.exp(jnp.sum(lq1_ref[...] * lk1_ref[...], axis=-1, keepdims=True))
    e2 = jnp.exp(jnp.sum(lq2_ref[...] * lk2_ref[...], axis=-1, keepdims=True))
    return e1 - e2 + LAM_INIT


ATTN_SUB = 64
FAST_SCORE_MAX = 30.0
Q_BLOCKS_PER_STEP = 2
SCORE_BOUND_PER_GAIN = 1.02 * LOG2E * DA_DK ** 0.5


def _lane_tile(x, width):
    return jnp.concatenate([x] * (width // LANES), axis=1)


def _attn_q_block(half, q_ref, k_ref, v_ref, lq1_ref, lk1_ref, lq2_ref, lk2_ref, gd_ref, dmask_ref,
                  o_ref, m_ref, l_ref, acc_ref, sa_ref, sb_ref, alpha_ref, p_ref, *, tk):
    tq = 2 * tk
    static_half = isinstance(half, int)
    q_lo = half * tq if static_half else pl.multiple_of(half * tq, tq)
    qi = Q_BLOCKS_PER_STEP * pl.program_id(2) + half

    def probs(kb, slot, diag=None, r0=0, first=False):
        start = pl.multiple_of(kb * tk, tk)
        for a in range(2):
            buf = 2 * slot + a
            q = q_ref[pl.ds(q_lo + r0, tq - r0), a * DA_DK:(a + 1) * DA_DK]
            k = k_ref[pl.ds(start, tk), a * DA_DK:(a + 1) * DA_DK]
            s = lax.dot_general(q, k, NT_DIMS, preferred_element_type=f32)
            if diag == 0:
                s = jnp.concatenate([s[:tk] + dmask_ref[...], s[tk:]], axis=0)
            elif diag == 1:
                s = s + dmask_ref[...]
            p = jnp.exp2(s)
            part = p[:, 0:LANES]
            for c in range(1, tk // LANES):
                part = part + p[:, c * LANES:(c + 1) * LANES]
            l_ref[a, r0:, :] = part if first else l_ref[a, r0:, :] + part
            p_ref[buf, r0:, :] = p.astype(bf16)

    def pv_add(kb, slot, r0=0):
        start = pl.multiple_of(kb * tk, tk)
        v = v_ref[pl.ds(start, tk), :]
        for a in range(2):
            acc_ref[a, r0:, :] = acc_ref[a, r0:, :] + jnp.dot(
                p_ref[2 * slot + a, r0:, :], v, preferred_element_type=f32)

    def fast_path():
        def pair(i, last_diag):
            probs(2 * i + 1, 1)
            pv_add(2 * i, 0)
            probs(2 * i + 2, 0, diag=last_diag)
            pv_add(2 * i + 1, 1)

        def first_block_only():
            probs(0, 0, diag=0, first=True)
            acc_ref[...] = jnp.zeros(acc_ref.shape, f32)

        def full_blocks_first():
            probs(0, 0, first=True)
            acc_ref[...] = jnp.zeros(acc_ref.shape, f32)

            def body(i, carry):
                pair(2 * i, None)
                pair(2 * i + 1, None)
                return carry

            n_pairs = qi - 1
            lax.fori_loop(0, lax.shift_right_logical(n_pairs, 1), body, 0)

            @pl.when(lax.bitwise_and(n_pairs, 1) == 1)
            def _():
                pair(n_pairs - 1, None)

            pair(qi - 1, 0)

        if not static_half or half == 0:
            pl.when(qi == 0)(first_block_only)
            pl.when(qi > 0)(full_blocks_first)
        else:
            full_blocks_first()

        pv_add(2 * qi, 0)
        probs(2 * qi + 1, 1, diag=1, r0=tk)
        pv_add(2 * qi + 1, 1, r0=tk)
        for a in range(2):
            l_ref[a] = jnp.broadcast_to(jnp.sum(l_ref[a], axis=1, keepdims=True), l_ref.shape[1:])


    def scores(kb, s_ref, r0=0):
        start = pl.multiple_of(kb * tk, tk)
        for a in range(2):
            q = q_ref[pl.ds(q_lo + r0, tq - r0), a * DA_DK:(a + 1) * DA_DK]
            k = k_ref[pl.ds(start, tk), a * DA_DK:(a + 1) * DA_DK]
            s_ref[a, r0:, :] = lax.dot_general(q, k, NT_DIMS, preferred_element_type=f32)

    def softmax(slot, diag=None, r0=0):
        s_ref = (sa_ref, sb_ref)[slot]
        for a in range(2):
            buf = 2 * slot + a
            for r in range(r0 // ATTN_SUB, tq // ATTN_SUB):
                rows = slice(r * ATTN_SUB, (r + 1) * ATTN_SUB)
                s = s_ref[a, rows, :]
                if diag is not None and (diag * tk + tk - 1) // CHUNK > (r * ATTN_SUB) // CHUNK:
                    row = (lax.broadcasted_iota(jnp.int32, s.shape, 0) + r * ATTN_SUB) // CHUNK
                    col = (lax.broadcasted_iota(jnp.int32, s.shape, 1) + diag * tk) // CHUNK
                    s = jnp.where(col <= row, s, -1e30)
                m_prev = m_ref[a, rows, :]
                m_new = jnp.maximum(m_prev, jnp.max(s, axis=1, keepdims=True))
                alpha = jnp.exp2(m_prev - m_new)
                p = jnp.exp2(s - _lane_tile(m_new, tk))
                l_ref[a, rows, :] = alpha * l_ref[a, rows, :] + jnp.sum(p, axis=1, keepdims=True)
                m_ref[a, rows, :] = m_new
                alpha_ref[buf, rows, :] = alpha
                p_ref[buf, rows, :] = p.astype(bf16)

    def pv(kb, slot, r0=0):
        start = pl.multiple_of(kb * tk, tk)
        v = v_ref[pl.ds(start, tk), :]
        for a in range(2):
            buf = 2 * slot + a
            acc_ref[a, r0:, :] = acc_ref[a, r0:, :] * _lane_tile(alpha_ref[buf, r0:, :], DA_DV) + jnp.dot(
                p_ref[buf, r0:, :], v, preferred_element_type=f32)

    def safe_path():
        m_ref[...] = jnp.full(m_ref.shape, -1e30, f32)
        l_ref[...] = jnp.zeros(l_ref.shape, f32)
        acc_ref[...] = jnp.zeros(acc_ref.shape, f32)
        alpha_ref[2:4] = jnp.ones((2,) + alpha_ref.shape[1:], f32)
        p_ref[2:4] = jnp.zeros((2,) + p_ref.shape[1:], bf16)
        scores(0, sa_ref)

        def body(i, carry):
            kb = 2 * i
            pv(jnp.maximum(kb - 1, 0), 1)
            scores(kb + 1, sb_ref)
            softmax(0)
            pv(kb, 0)
            scores(kb + 2, sa_ref)
            softmax(1)
            return carry

        lax.fori_loop(0, qi, body, 0)
        pv(jnp.maximum(2 * qi - 1, 0), 1)
        scores(2 * qi + 1, sb_ref, r0=tk)
        softmax(0, diag=0)
        pv(2 * qi, 0)
        softmax(1, diag=1, r0=tk)
        pv(2 * qi + 1, 1, r0=tk)

    def finalize():
        lam = _lam(lq1_ref, lk1_ref, lq2_ref, lk2_ref)
        o = acc_ref[0] * _lane_tile(1.0 / l_ref[0], DA_DV) - acc_ref[1] * _lane_tile(lam / l_ref[1], DA_DV)
        o_ref[pl.ds(q_lo, tq), :] = (_rms(o) * (gd_ref[...] * (1.0 - LAM_INIT))).astype(bf16)

    return fast_path, safe_path, finalize


def _attn_prompt_kernel(q_ref, k_ref, v_ref, lq1_ref, lk1_ref, lq2_ref, lk2_ref, gd_ref, gq_ref, gk_ref,
                        dmask_ref, o_ref, *scratch, tk):
    gq_max = jnp.max(jnp.abs(gq_ref[...]), axis=1, keepdims=True)
    gk_max = jnp.max(jnp.abs(gk_ref[...]), axis=1, keepdims=True)
    fast = (gq_max * gk_max)[0, 0] * SCORE_BOUND_PER_GAIN <= FAST_SCORE_MAX
    blocks = [_attn_q_block(half, q_ref, k_ref, v_ref, lq1_ref, lk1_ref, lq2_ref, lk2_ref, gd_ref,
                            dmask_ref, o_ref, *scratch, tk=tk) for half in range(Q_BLOCKS_PER_STEP)]

    @pl.when(fast)
    def _():
        for fast_path, _, finalize in blocks:
            fast_path()
            finalize()

    @pl.when(jnp.logical_not(fast))
    def _():
        def one_block(half, carry):
            _, safe_path, finalize = _attn_q_block(half, q_ref, k_ref, v_ref, lq1_ref, lk1_ref, lq2_ref,
                                                   lk2_ref, gd_ref, dmask_ref, o_ref, *scratch, tk=tk)
            safe_path()
            finalize()
            return carry

        lax.fori_loop(0, Q_BLOCKS_PER_STEP, one_block, 0)


def _attn_prompt(dq, dkb, dvb, lam_params, g_dnorm, g_qn, g_kn, *, nb, seq, tk):
    hw = 2 * DA_DK
    tq = 2 * tk
    rows = Q_BLOCKS_PER_STEP * tq
    qspec = pl.BlockSpec((rows, hw), lambda b, h, i: (b * (seq // rows) + i, h))
    kvspec = pl.BlockSpec((seq, hw), lambda b, h, i: (b, h))
    vec = _const_spec((1, DA_DK))
    chunk_of = jnp.arange(tk) // CHUNK
    dmask = jnp.where(chunk_of[None, :] <= chunk_of[:, None], 0.0, -1e30).astype(f32)
    return pl.pallas_call(
        functools.partial(_attn_prompt_kernel, tk=tk),
        grid=(nb, DA_HEADS, seq // rows),
        in_specs=[qspec, kvspec, kvspec, vec, vec, vec, vec, _const_spec((1, DA_DV)), vec, vec,
                  _const_spec((tk, tk))],
        out_specs=qspec,
        out_shape=jax.ShapeDtypeStruct(dq.shape, bf16),
        scratch_shapes=[pltpu.VMEM((2, tq, LANES), f32), pltpu.VMEM((2, tq, LANES), f32),
                        pltpu.VMEM((2, tq, DA_DV), f32), pltpu.VMEM((2, tq, tk), f32),
                        pltpu.VMEM((2, tq, tk), f32), pltpu.VMEM((4, tq, LANES), f32),
                        pltpu.VMEM((4, tq, tk), bf16)],
        compiler_params=_params(("parallel", "parallel", "arbitrary")),
        name="attn_prompt",
    )(dq, dkb, dvb, *lam_params, g_dnorm, g_qn, g_kn, dmask)


def _attn_decode_kernel(q_ref, kc_ref, vc_ref, kn_ref, vn_ref, lq1_ref, lk1_ref, lq2_ref, lk2_ref,
                        gd_ref, o_ref, *, past):
    lam = _lam(lq1_ref, lk1_ref, lq2_ref, lk2_ref)
    rows_per_token = DA_HEADS * 2
    for h in range(DA_HEADS):
        hs = slice(h * DA_DV, (h + 1) * DA_DV)
        probs = []
        for a in range(2):
            cols = slice(h * DA_DV + a * DA_DK, h * DA_DV + (a + 1) * DA_DK)
            q = q_ref[:, cols]
            kc = kc_ref[pl.ds(h * 2 + a, past, stride=rows_per_token), :].astype(bf16)
            kn = kn_ref[:, cols]
            sc = lax.dot_general(q, kc, NT_DIMS, preferred_element_type=f32)
            sn = lax.dot_general(q, kn, NT_DIMS, preferred_element_type=f32)
            m = jnp.maximum(jnp.max(sc, axis=1, keepdims=True), jnp.max(sn, axis=1, keepdims=True))
            pc = jnp.exp2(sc - m)
            pn = jnp.exp2(sn - m)
            inv = 1.0 / (jnp.sum(pc, axis=1, keepdims=True) + jnp.sum(pn, axis=1, keepdims=True))
            probs.append((pc * inv, pn * inv))
        ac = probs[0][0] - lam * probs[1][0]
        an = probs[0][1] - lam * probs[1][1]
        vc = jnp.concatenate([vc_ref[pl.ds(half * DA_HEADS + h, past, stride=rows_per_token), :]
                              for half in range(2)], axis=1).astype(bf16)
        o = jnp.dot(ac.astype(bf16), vc, preferred_element_type=f32)
        o = o + jnp.dot(an.astype(bf16), vn_ref[:, hs], preferred_element_type=f32)
        o_ref[:, hs] = (_rms(o) * gd_ref[...] * (1.0 - LAM_INIT)).astype(bf16)


def _attn_decode(dq_v, dkb_v, dvb_v, cache_k_rows, cache_v_rows, lam_params, g_dnorm, *, nb, t, past):
    new = pl.BlockSpec((t, IN_COLS), lambda b: (0, b))
    old_k = pl.BlockSpec((past * DA_HEADS * 2, DA_DK), lambda b: (b, 0))
    old_v = old_k
    vec = _const_spec((1, DA_DK))
    return pl.pallas_call(
        functools.partial(_attn_decode_kernel, past=past),
        grid=(nb,),
        in_specs=[new, old_k, old_v, new, new, vec, vec, vec, vec, _const_spec((1, DA_DV))],
        out_specs=new,
        out_shape=jax.ShapeDtypeStruct(dq_v.shape, bf16),
        compiler_params=_params(("parallel",)),
        name="attn_decode",
    )(dq_v, cache_k_rows, cache_v_rows, dkb_v, dvb_v, *lam_params, g_dnorm)


def _merge_kernel(x_ref, oa_ref, ob_ref, ga_ref, gb_ref, wr_ref, wd_ref, wo_ref, y_ref):
    a = jnp.dot(oa_ref[...], wr_ref[...], preferred_element_type=f32)
    b = jnp.dot(ob_ref[...], wd_ref[...], preferred_element_type=f32)
    m = ga_ref[...].astype(f32) * a + gb_ref[...].astype(f32) * b
    y_ref[...] = x_ref[...] + jnp.dot(m.astype(bf16), wo_ref[...], preferred_element_type=f32)


def _merge(x2d, oa, ob, ga, gb, w_ret_br, w_da_br, w_out, *, tm):
    n = x2d.shape[0]
    row = pl.BlockSpec((tm, D_MODEL), lambda i: (i, 0))
    w = _const_spec((D_MODEL, D_MODEL))
    return pl.pallas_call(
        _merge_kernel,
        grid=(n // tm,),
        in_specs=[row] * 5 + [w] * 3,
        out_specs=row,
        out_shape=jax.ShapeDtypeStruct(x2d.shape, f32),
        compiler_params=_params(("parallel",)),
        name="merge",
    )(x2d, oa, ob, ga, gb, w_ret_br, w_da_br, w_out)


FF_CHUNK = 256


U_CHUNKS = 2 * D_FF // LANES


def _ffn_kernel(x_ref, ple_ref, cst_ref, gffn_ref, wup_ref, cw_ref, cb_ref, wdn_ref, gple_ref,
                wpg_ref, wple_ref, y_ref, cnew_ref, uext_ref, act_ref, *, tm, shift, halo, chunk_rows):
    first = pl.program_id(1) == 0
    x1 = x_ref[...]
    xn = (_rms(x1) * gffn_ref[...]).astype(bf16)
    state_rows = cst_ref.shape[1]
    lane_chunks = FF_CHUNK // LANES

    def chunk_slab(t0, c, rows):
        return pl.ds((halo + t0) * U_CHUNKS + c, rows, stride=U_CHUNKS)

    @pl.when(first)
    def _():
        if chunk_rows:
            for c in range(U_CHUNKS):
                uext_ref[chunk_slab(-halo, c, halo), :] = cst_ref[0, state_rows - halo:, c * LANES:(c + 1) * LANES]
        else:
            uext_ref[0:halo, :] = cst_ref[0]

    def up(col):
        ucol = jnp.dot(xn, wup_ref[:, col:col + FF_CHUNK], preferred_element_type=f32)
        if chunk_rows:
            for k in range(lane_chunks):
                uext_ref[chunk_slab(0, col // LANES + k, tm), :] = ucol[:, k * LANES:(k + 1) * LANES]
            cnew_ref[0, 0:state_rows - halo, col:col + FF_CHUNK] = jnp.zeros((state_rows - halo, FF_CHUNK), f32)
            cnew_ref[0, state_rows - halo:, col:col + FF_CHUNK] = ucol[tm - halo:, :]
        else:
            uext_ref[halo:halo + tm, col:col + FF_CHUNK] = ucol

    def tap(back, col):
        if chunk_rows:
            return jnp.concatenate([uext_ref[chunk_slab(-back * shift, col // LANES + k, tm), :]
                                    for k in range(lane_chunks)], axis=1)
        return uext_ref[halo - back * shift:halo - back * shift + tm, col:col + FF_CHUNK]

    def conv(col):
        sl = slice(col, col + FF_CHUNK)
        acc = cw_ref[0:1, sl] * tap(2, col)
        acc = acc + cw_ref[1:2, sl] * tap(1, col)
        acc = acc + cw_ref[2:3, sl] * tap(0, col)
        return acc + cb_ref[:, sl]

    for col in range(0, D_FF, FF_CHUNK):
        up(col)
        up(D_FF + col)
        act_ref[:, col:col + FF_CHUNK] = (_silu(conv(col)) * conv(D_FF + col)).astype(bf16)

    x2 = x1 + jnp.dot(act_ref[...], wdn_ref[...], preferred_element_type=f32)
    if chunk_rows:
        keep = halo * U_CHUNKS
        uext_ref[0:keep, :] = uext_ref[tm * U_CHUNKS:tm * U_CHUNKS + keep, :]
    else:
        tail = uext_ref[tm:tm + halo, :]
        cnew_ref[0] = tail
        uext_ref[0:halo, :] = tail

    gate = jnp.dot((_rms(x2) * gple_ref[...]).astype(bf16), wpg_ref[...], preferred_element_type=f32)
    emb = jnp.dot(ple_ref[...].astype(bf16), wple_ref[...], preferred_element_type=f32)
    y_ref[...] = x2 + _sigmoid(gate) * emb


def _ffn(x1, ple2d, cstate, g_ffn, w_up, conv_w, conv_b, w_down, g_ple, w_ple_gate, w_ple, *,
         tm, shift):
    n = x1.shape[0]
    nseg, crows, _ = cstate.shape
    chunk_rows = shift % SUBLANES != 0
    halo = (CONV_W - 1) * shift
    tiles = n // (nseg * tm)
    ple_dim = ple2d.shape[1]
    resident = _resident
    row = lambda width: pl.BlockSpec((tm, width), lambda s, t: (s * tiles + t, 0))
    seg = pl.BlockSpec((1, crows, 2 * D_FF), lambda s, t: (s, 0, 0))
    uext = ((halo + tm) * U_CHUNKS, LANES) if chunk_rows else (halo + tm, 2 * D_FF)
    return pl.pallas_call(
        functools.partial(_ffn_kernel, tm=tm, shift=shift, halo=halo, chunk_rows=chunk_rows),
        grid=(nseg, tiles),
        in_specs=[row(D_MODEL), row(ple_dim), seg, resident((1, D_MODEL)),
                  resident((D_MODEL, 2 * D_FF)), resident((CONV_W, 2 * D_FF)),
                  resident((1, 2 * D_FF)), resident((D_FF, D_MODEL)), resident((1, D_MODEL)),
                  resident((D_MODEL, D_MODEL)), resident((ple_dim, D_MODEL))],
        out_specs=[row(D_MODEL), seg],
        out_shape=[jax.ShapeDtypeStruct(x1.shape, f32), jax.ShapeDtypeStruct(cstate.shape, f32)],
        scratch_shapes=[pltpu.VMEM(uext, f32), pltpu.VMEM((tm, D_FF), bf16)],
        compiler_params=_params(("parallel", "arbitrary")),
        name="ffn",
    )(x1, ple2d, cstate, g_ffn, w_up, conv_w, conv_b, w_down, g_ple, w_ple_gate, w_ple)


def _rope_freqs():
    return ROPE_THETA ** (-jnp.arange(0, RET_DK, 2, dtype=f32) / RET_DK)


def _rope_finish(cos_half, sin_half):
    sign = jnp.where(jnp.arange(RET_DK) < RET_DK // 2, -1.0, 1.0).astype(f32)
    return (jnp.concatenate([cos_half, cos_half], axis=-1),
            jnp.concatenate([sin_half, sin_half], axis=-1) * sign)


def _rope_tables(pos):
    ang = pos.astype(f32)[:, None] * _rope_freqs()[None, :]
    return _rope_finish(jnp.cos(ang), jnp.sin(ang))


ROPE_SPLIT = 64


def _rope_tables_range(seq):
    inv = _rope_freqs()[None, :]
    hi = (ROPE_SPLIT * jnp.arange(seq // ROPE_SPLIT, dtype=f32))[:, None] * inv
    lo = jnp.arange(ROPE_SPLIT, dtype=f32)[:, None] * inv
    ch, sh = jnp.cos(hi)[:, None, :], jnp.sin(hi)[:, None, :]
    cl, sl = jnp.cos(lo)[None, :, :], jnp.sin(lo)[None, :, :]
    cos =(ch * cl - sh * sl).reshape(seq, RET_DK // 2)
    sin = (sh * cl + ch * sl).reshape(seq, RET_DK // 2)
    return _rope_finish(cos, sin)


def _layer_prompt(x, ple, w, *, tm_in=512, tm_merge=1024, tm_ffn=512, ret_chunk=256, attn_tk=512):
    nb, seq, _ = x.shape
    n = nb * seq
    tm_in, tm_merge, tm_ffn = min(tm_in, seq), min(tm_merge, seq), min(tm_ffn, seq)
    ret_chunk, attn_tk = min(ret_chunk, seq), min(attn_tk, seq // (2 * Q_BLOCKS_PER_STEP))
    x2d = x.reshape(n, D_MODEL)
    cos, sin = _rope_tables_range(seq)
    rqk, rv, rg, dq, dk, dkb, dv, dvb, ga, gb = _in_proj(
        x2d, w["g_mix"], w["w_in"], cos, sin, w["g_qn"], w["g_kn"], tm=tm_in)

    n_chunks = seq // ret_chunk
    s0 = jnp.zeros((nb, RET_HEADS, RET_DK, RET_DV), f32)
    per_stream = lambda a: a.reshape(nb, seq, IN_COLS)
    oa, s_new = _retention(per_stream(rqk), per_stream(rv), per_stream(rg), s0, chunk=ret_chunk,
                           n_chunks=n_chunks, stream_major=True)
    oa = oa.reshape(n, IN_COLS)
    ob = _attn_prompt(dq, dkb, dvb, w["lam"], w["g_dnorm"], w["g_qn"], w["g_kn"],
                      nb=nb, seq=seq, tk=attn_tk)
    x1 = _merge(x2d, oa, ob, ga, gb, w["w_ret_br"], w["w_da_br"], w["w_out"], tm=tm_merge)

    cstate = jnp.zeros((nb, SUBLANES, 2 * D_FF), f32)
    y, ctail = _ffn(x1, ple.reshape(n, -1), cstate, w["g_ffn"], w["w_up"], w["conv_w"], w["conv_b"],
                    w["w_down"], w["g_ple"], w["w_ple_gate"], w["w_ple"], tm=tm_ffn, shift=1)
    return (y.reshape(nb, seq, D_MODEL),
            dk.reshape(nb, seq, DA_HEADS, 2, DA_DK),
            jnp.swapaxes(dv.reshape(nb, seq, 2, DA_HEADS, LANES), 2, 3).reshape(nb, seq, DA_HEADS, DA_DV),
            s_new, ctail[:, SUBLANES - (CONV_W - 1):])


def _layer_decode(x, ple, cache_k, cache_v, state_ret, state_conv, w, *, past_len):
    nb, t, _ = x.shape
    n = nb * t
    past = cache_k.shape[1]
    tm = n

    def to_tm(a):
        return jnp.swapaxes(a, 0, 1).reshape(n, a.shape[-1])

    def from_tm(a):
        return jnp.swapaxes(a.reshape(t, nb, a.shape[-1]), 0, 1)

    cos, sin = _rope_tables(past_len + jnp.arange(n) // nb)
    rqk, rv, rg, dq, dk, dkb, dv, dvb, ga, gb = _in_proj(
        to_tm(x), w["g_mix"], w["w_in"], cos, sin, w["g_qn"], w["g_kn"], tm=tm)

    view = lambda a: a.reshape(t, nb * IN_COLS)
    oa, s_new = _retention(view(rqk), view(rv), view(rg), state_ret, chunk=t, n_chunks=1,
                           stream_major=False)
    ob = _attn_decode(view(dq), view(dkb), view(dvb),
                      cache_k.reshape(nb * past * DA_HEADS * 2, DA_DK),
                      jnp.swapaxes(cache_v.reshape(nb, past, DA_HEADS, 2, LANES), 2, 3).reshape(
                          nb * past * DA_HEADS * 2, LANES),
                      w["lam"], w["g_dnorm"], nb=nb, t=t, past=past)
    x1 = _merge(to_tm(x), oa.reshape(n, IN_COLS), ob.reshape(n, IN_COLS), ga, gb,
                w["w_ret_br"], w["w_da_br"], w["w_out"], tm=tm)

    halo = (CONV_W - 1) * nb
    cstate = jnp.swapaxes(state_conv, 0, 1).reshape(1, halo, 2 * D_FF)
    y, ctail = _ffn(x1, to_tm(ple), cstate, w["g_ffn"], w["w_up"], w["conv_w"], w["conv_b"],
                    w["w_down"], w["g_ple"], w["w_ple_gate"], w["w_ple"], tm=tm, shift=nb)
    return (from_tm(y),
            jnp.swapaxes(dk.reshape(t, nb, DA_HEADS, 2, DA_DK), 0, 1),
            jnp.transpose(dv.reshape(t, nb, 2, DA_HEADS, LANES), (1, 0, 3, 2, 4)).reshape(
                nb, t, DA_HEADS, DA_DV),
            s_new, jnp.swapaxes(ctail.reshape(CONV_W - 1, nb, 2 * D_FF), 0, 1))


def _layer_weights(i, g_mix, w_in, g_qn, g_kn, lam_q1, lam_k1, lam_q2, lam_k2, g_dnorm, w_ret_br,
                   w_da_br, w_out, g_ffn, w_up, conv_w, conv_b, w_down, g_ple, w_ple_gate, w_ple):
    vec = lambda a: a[i].reshape(1, -1)
    mats = {k: v[i].astype(bf16) for k, v in dict(
        w_in=w_in, w_ret_br=w_ret_br, w_da_br=w_da_br, w_out=w_out, w_up=w_up, w_down=w_down,
        w_ple_gate=w_ple_gate, w_ple=w_ple).items()}
    return dict(
        g_mix=vec(g_mix), g_qn=vec(g_qn), g_kn=vec(g_kn),
        lam=(vec(lam_q1), vec(lam_k1), vec(lam_q2), vec(lam_k2)), g_dnorm=vec(g_dnorm),
        g_ffn=vec(g_ffn), conv_w=conv_w[i], conv_b=vec(conv_b), g_ple=vec(g_ple), **mats)


def kernel(x_prompt, x_sample, cache_k, cache_v, state_ret, state_conv, p_prompt, p_sample, g_mix, w_in, g_qn, g_kn, lam_q1, lam_k1, lam_q2, lam_k2, g_dnorm, w_ret_br, w_da_br, w_out, g_ffn, w_up, conv_w, conv_b, w_down, g_ple, w_ple_gate, w_ple):
    depth = w_in.shape[0]
    yp, ys = x_prompt, x_sample
    outs = [[] for _ in range(8)]
    for i in range(depth):
        w = _layer_weights(i, g_mix, w_in, g_qn, g_kn, lam_q1, lam_k1, lam_q2, lam_k2, g_dnorm,
                           w_ret_br, w_da_br, w_out, g_ffn, w_up, conv_w, conv_b, w_down, g_ple,
                           w_ple_gate, w_ple)
        yp, k1, v1, s1, c1 = _layer_prompt(yp, p_prompt[i], w)
        ys, k2, v2, s2, c2 = _layer_decode(ys, p_sample[i], cache_k[i], cache_v[i], state_ret[i],
                                           state_conv[i], w, past_len=PAST_LEN)
        for lst, val in zip(outs, (k1, k2, v1, v2, s1, s2, c1, c2)):
            lst.append(val)
    return (yp, ys) + tuple(jnp.stack(o) for o in outs)
```

```python
import functools
import math

import jax
import jax.numpy as jnp
import numpy as np
from jax import lax
from jax.experimental import pallas as pl
from jax.experimental.pallas import tpu as pltpu

D_MODEL = 1024
CHUNK = 64
RET_HEADS = 4
RET_DK = 128
RET_DV = 256
DA_HEADS = 4
DA_DK = 128
DA_DV = 256
D_FF = 2816
CONV_W = 3
ROPE_THETA = 10000.0
EPS = 1e-6
PAST_LEN = 1024
LAM_INIT = 0.8 - 0.6 * math.exp(-0.3 * 0)
LOG2E = math.log2(math.e)

LANES = 128
SUBLANES = 8
IN_COLS = 1024
N_IN_BLOCKS = 8
VMEM_LIMIT = 56 * 1024 * 1024

f32 = jnp.float32
bf16 = jnp.bfloat16

NT_DIMS = (((1,), (1,)), ((), ()))
TN_DIMS = (((0,), (0,)), ((), ()))


def _rms(x):
    return x * lax.rsqrt(jnp.mean(x * x, axis=-1, keepdims=True) + EPS)


def _rope(x, cos, sin_signed):
    return x * cos + pltpu.roll(x, RET_DK // 2, 1) * sin_signed


def _sigmoid(x):
    return 0.5 * jnp.tanh(0.5 * x) + 0.5


def _silu(x):
    h = 0.5 * x
    return h * jnp.tanh(h) + h


def _params(sem):
    return pltpu.CompilerParams(dimension_semantics=sem, vmem_limit_bytes=VMEM_LIMIT)


def _const_spec(shape):
    nd = len(shape)
    return pl.BlockSpec(shape, lambda *_: (0,) * nd)


def _in_proj_kernel(x_ref, gmix_ref, w_ref, cos_ref, sin_ref, gq_ref, gk_ref,
                    rqk_ref, rv_ref, rg_ref, dq_ref, dk_ref, dkb_ref, dv_ref, dvb_ref,
                    ga_ref, gb_ref):
    half = x_ref.shape[0] // 2
    h_top = (_rms(x_ref[0:half, :]) * gmix_ref[...]).astype(bf16)
    h_bot = (_rms(x_ref[half:, :]) * gmix_ref[...]).astype(bf16)
    h = jnp.concatenate([h_top, h_bot], axis=0)
    cos = cos_ref[...]
    sin = sin_ref[...]
    n_heads = IN_COLS // LANES

    def block(j):
        return jnp.dot(h, w_ref[:, j * IN_COLS:(j + 1) * IN_COLS], preferred_element_type=f32)

    def heads(z):
        return [(slice(g * LANES, (g + 1) * LANES), z[:, g * LANES:(g + 1) * LANES])
                for g in range(n_heads)]

    tm = x_ref.shape[0]
    w_dk = w_ref[:, 4 * IN_COLS:5 * IN_COLS]
    z_dk = jnp.concatenate([jnp.dot(h_top, w_dk, preferred_element_type=f32),
                            jnp.dot(h_bot, w_dk, preferred_element_type=f32)], axis=0)
    for g, (sl, zg) in enumerate(heads(z_dk)):
        r = _rope(_rms(zg) * gk_ref[...], cos, sin)
        dk_ref[pl.ds(g, tm, stride=n_heads), :] = r
        dkb_ref[:, sl] = r.astype(bf16)

    for sl, zg in heads(block(3)):
        r = _rope(_rms(zg) * gq_ref[...], cos, sin) * (DA_DK ** -0.5 * LOG2E)
        dq_ref[:, sl] = r.astype(bf16)

    for g, (sl, zg) in enumerate(heads(block(0))):
        r = _rope(zg, cos, sin)
        if g >= RET_HEADS:
            r = r * (RET_DK ** -0.5)
        rqk_ref[:, sl] = r.astype(bf16)

    z = block(5)
    for g, (sl, zg) in enumerate(heads(z)):
        head, half = divmod(g, 2)
        dv_ref[pl.ds(half * DA_HEADS + head, tm, stride=n_heads), :] = zg
    dvb_ref[...] = z.astype(bf16)
    z = block(2)
    rg_ref[...] = _silu(z).astype(bf16)
    ga_ref[...] = _sigmoid(block(6)).astype(bf16)
    gb_ref[...] = _sigmoid(block(7)).astype(bf16)
    rv_ref[...] = block(1).astype(bf16)


def _resident(shape):
    nd = len(shape)
    return pl.BlockSpec(shape, lambda *_: (0,) * nd, pipeline_mode=pl.Buffered(1))


def _in_proj(x2d, g_mix, w_in, cos, sin, g_qn, g_kn, *, tm):
    n = x2d.shape[0]
    pos_blocks = cos.shape[0] // tm
    row = pl.BlockSpec((tm, IN_COLS), lambda i: (i, 0))
    tab = pl.BlockSpec((tm, LANES), lambda i: (i % pos_blocks, 0))
    bf = jax.ShapeDtypeStruct((n, IN_COLS), bf16)
    chunks = IN_COLS // LANES
    fl = jax.ShapeDtypeStruct((n * chunks, LANES), f32)
    lin = pl.BlockSpec((tm * chunks, LANES), lambda i: (i, 0))
    return pl.pallas_call(
        _in_proj_kernel,
        grid=(n // tm,),
        in_specs=[row, _resident((1, D_MODEL)), _resident((D_MODEL, N_IN_BLOCKS * IN_COLS)),
                  tab, tab, _resident((1, LANES)), _resident((1, LANES))],
        out_specs=[row, row, row, row, lin, row, lin, row, row, row],
        out_shape=[bf, bf, bf, bf, fl, bf, fl, bf, bf, bf],
        compiler_params=_params(("parallel",)),
        name="in_proj",
    )(x2d, g_mix, w_in, cos, sin, g_qn, g_kn)


def _ret_log_gamma():
    return np.log(1.0 - np.exp2(-5.0 - np.arange(RET_HEADS, dtype=np.float64)))


def _ret_tables(c):
    log_g = jnp.asarray(_ret_log_gamma(), f32)[:, None, None]
    idx = jnp.arange(c, dtype=f32)
    diff = idx[:, None] - idx[None, :]
    decay = jnp.where(diff >= 0, jnp.exp(log_g * jnp.maximum(diff, 0.0)), 0.0)
    qd = jnp.broadcast_to(jnp.exp(log_g * (idx + 1.0)[None, :, None]), (RET_HEADS, c, RET_DV))
    kd = jnp.broadcast_to(jnp.exp(log_g * (c - 1.0 - idx)[None, :, None]), (RET_HEADS, c, RET_DK))
    return decay, qd, kd


RET_STREAMS = 4


def _retention_kernel(qk_ref, v_ref, g_ref, dec_ref, qd_ref, kd_ref, s0_ref, o_ref, s_ref, *,
                      chunk, stream_major):
    @pl.when(pl.program_id(1) == 0)
    def _():
        s_ref[...] = s0_ref[...]

    def cols(ref, s, lo, width):
        if stream_major:
            return ref[s, :, lo:lo + width]
        return ref[:, s * IN_COLS + lo:s * IN_COLS + lo + width]

    log_g = _ret_log_gamma()
    for s in range(s_ref.shape[0]):
        for h in range(RET_HEADS):
            q = cols(qk_ref, s, h * RET_DK, RET_DK)
            k = cols(qk_ref, s, (RET_HEADS + h) * RET_DK, RET_DK)
            v = cols(v_ref, s, h * RET_DV, RET_DV)
            state = s_ref[s, h]
            scores = lax.dot_general(q, k, NT_DIMS, preferred_element_type=f32) * dec_ref[h]
            intra = jnp.dot(scores.astype(bf16), v, preferred_element_type=f32)
            inter = jnp.dot(q, state.astype(bf16), preferred_element_type=f32) * qd_ref[h]
            kw = (k.astype(f32) * kd_ref[h]).astype(bf16)
            s_ref[s, h] = float(np.exp(chunk * log_g[h])) * state + lax.dot_general(
                kw, v, TN_DIMS, preferred_element_type=f32)
            o = _rms(intra + inter)
            gate = cols(g_ref, s, h * RET_DV, RET_DV).astype(f32)
            out = (o * gate).astype(bf16)
            if stream_major:
                o_ref[s, :, h * RET_DV:(h + 1) * RET_DV] = out
            else:
                o_ref[:, s * IN_COLS + h * RET_DV:s * IN_COLS + (h + 1) * RET_DV] = out


def _retention(rqk, rv, rg, s0, *, chunk, n_chunks, stream_major):
    nb = s0.shape[0]
    streams = math.gcd(nb, RET_STREAMS)
    decay, qd, kd = _ret_tables(chunk)
    if stream_major:
        blk = pl.BlockSpec((streams, chunk, IN_COLS), lambda b, c: (b, c, 0))
    else:
        blk = pl.BlockSpec((chunk, streams * IN_COLS), lambda b, c: (0, b))
    st = pl.BlockSpec((streams, RET_HEADS, RET_DK, RET_DV), lambda b, c: (b, 0, 0, 0))
    return pl.pallas_call(
        functools.partial(_retention_kernel, chunk=chunk, stream_major=stream_major),
        grid=(nb // streams, n_chunks),
        in_specs=[blk, blk, blk, _const_spec(decay.shape), _const_spec(qd.shape),
                  _const_spec(kd.shape), st],
        out_specs=[blk, st],
        out_shape=[jax.ShapeDtypeStruct(rqk.shape, bf16), jax.ShapeDtypeStruct(s0.shape, f32)],
        compiler_params=_params(("parallel", "arbitrary")),
        name="retention",
    )(rqk, rv, rg, decay, qd, kd, s0)


def _lam(lq1_ref, lk1_ref, lq2_ref, lk2_ref):
    e1 = jnp.exp(jnp.sum(lq1_ref[...] * lk1_ref[...], axis=-1, keepdims=True))
    e2 = jnp.exp(jnp.sum(lq2_ref[...] * lk2_ref[...], axis=-1, keepdims=True))
    return e1 - e2 + LAM_INIT


ATTN_SUB = 64
FAST_SCORE_MAX = 30.0
Q_BLOCKS_PER_STEP = 2
SCORE_BOUND_PER_GAIN = 1.02 * LOG2E * DA_DK ** 0.5


def _lane_tile(x, width):
    return jnp.concatenate([x] * (width // LANES), axis=1)


def _attn_q_block(half, q_ref, k_ref, v_ref, lq1_ref, lk1_ref, lq2_ref, lk2_ref, gd_ref, dmask_ref,
                  o_ref, m_ref, l_ref, acc_ref, sa_ref, sb_ref, alpha_ref, p_ref, *, tk):
    tq = 2 * tk
    static_half = isinstance(half, int)
    q_lo = half * tq if static_half else pl.multiple_of(half * tq, tq)
    qi = Q_BLOCKS_PER_STEP * pl.program_id(2) + half

    def probs(kb, slot, diag=None, r0=0, first=False):
        start = pl.multiple_of(kb * tk, tk)
        for a in range(2):
            buf = 2 * slot + a
            q = q_ref[pl.ds(q_lo + r0, tq - r0), a * DA_DK:(a + 1) * DA_DK]
            k = k_ref[pl.ds(start, tk), a * DA_DK:(a + 1) * DA_DK]
            s = lax.dot_general(q, k, NT_DIMS, preferred_element_type=f32)
            if diag == 0:
                s = jnp.concatenate([s[:tk] + dmask_ref[...], s[tk:]], axis=0)
            elif diag == 1:
                s = s + dmask_ref[...]
            p = jnp.exp2(s)
            part = p[:, 0:LANES]
            for c in range(1, tk // LANES):
                part = part + p[:, c * LANES:(c + 1) * LANES]
            l_ref[a, r0:, :] = part if first else l_ref[a, r0:, :] + part
            p_ref[buf, r0:, :] = p.astype(bf16)

    def pv_add(kb, slot, r0=0):
        start = pl.multiple_of(kb * tk, tk)
        v = v_ref[pl.ds(start, tk), :]
        for a in range(2):
            acc_ref[a, r0:, :] = acc_ref[a, r0:, :] + jnp.dot(
                p_ref[2 * slot + a, r0:, :], v, preferred_element_type=f32)

    def fast_path():
        def pair(i, last_diag):
            probs(2 * i + 1, 1)
            pv_add(2 * i, 0)
            probs(2 * i + 2, 0, diag=last_diag)
            pv_add(2 * i + 1, 1)

        def first_block_only():
            probs(0, 0, diag=0, first=True)
            acc_ref[...] = jnp.zeros(acc_ref.shape, f32)

        def full_blocks_first():
            probs(0, 0, first=True)
            acc_ref[...] = jnp.zeros(acc_ref.shape, f32)

            def body(i, carry):
                pair(2 * i, None)
                pair(2 * i + 1, None)
                return carry

            n_pairs = qi - 1
            lax.fori_loop(0, lax.shift_right_logical(n_pairs, 1), body, 0)

            @pl.when(lax.bitwise_and(n_pairs, 1) == 1)
            def _():
                pair(n_pairs - 1, None)

            pair(qi - 1, 0)

        if not static_half or half == 0:
            pl.when(qi == 0)(first_block_only)
            pl.when(qi > 0)(full_blocks_first)
        else:
            full_blocks_first()

        pv_add(2 * qi, 0)
        probs(2 * qi + 1, 1, diag=1, r0=tk)
        pv_add(2 * qi + 1, 1, r0=tk)
        for a in range(2):
            l_ref[a] = jnp.broadcast_to(jnp.sum(l_ref[a], axis=1, keepdims=True), l_ref.shape[1:])


    def scores(kb, s_ref, r0=0):
        start = pl.multiple_of(kb * tk, tk)
        for a in range(2):
            q = q_ref[pl.ds(q_lo + r0, tq - r0), a * DA_DK:(a + 1) * DA_DK]
            k = k_ref[pl.ds(start, tk), a * DA_DK:(a + 1) * DA_DK]
            s_ref[a, r0:, :] = lax.dot_general(q, k, NT_DIMS, preferred_element_type=f32)

    def softmax(slot, diag=None, r0=0):
        s_ref = (sa_ref, sb_ref)[slot]
        for a in range(2):
            buf = 2 * slot + a
            for r in range(r0 // ATTN_SUB, tq // ATTN_SUB):
                rows = slice(r * ATTN_SUB, (r + 1) * ATTN_SUB)
                s = s_ref[a, rows, :]
                if diag is not None and (diag * tk + tk - 1) // CHUNK > (r * ATTN_SUB) // CHUNK:
                    row = (lax.broadcasted_iota(jnp.int32, s.shape, 0) + r * ATTN_SUB) // CHUNK
                    col = (lax.broadcasted_iota(jnp.int32, s.shape, 1) + diag * tk) // CHUNK
                    s = jnp.where(col <= row, s, -1e30)
                m_prev = m_ref[a, rows, :]
                m_new = jnp.maximum(m_prev, jnp.max(s, axis=1, keepdims=True))
                alpha = jnp.exp2(m_prev - m_new)
                p = jnp.exp2(s - _lane_tile(m_new, tk))
                l_ref[a, rows, :] = alpha * l_ref[a, rows, :] + jnp.sum(p, axis=1, keepdims=True)
                m_ref[a, rows, :] = m_new
                alpha_ref[buf, rows, :] = alpha
                p_ref[buf, rows, :] = p.astype(bf16)

    def pv(kb, slot, r0=0):
        start = pl.multiple_of(kb * tk, tk)
        v = v_ref[pl.ds(start, tk), :]
        for a in range(2):
            buf = 2 * slot + a
            acc_ref[a, r0:, :] = acc_ref[a, r0:, :] * _lane_tile(alpha_ref[buf, r0:, :], DA_DV) + jnp.dot(
                p_ref[buf, r0:, :], v, preferred_element_type=f32)

    def safe_path():
        m_ref[...] = jnp.full(m_ref.shape, -1e30, f32)
        l_ref[...] = jnp.zeros(l_ref.shape, f32)
        acc_ref[...] = jnp.zeros(acc_ref.shape, f32)
        alpha_ref[2:4] = jnp.ones((2,) + alpha_ref.shape[1:], f32)
        p_ref[2:4] = jnp.zeros((2,) + p_ref.shape[1:], bf16)
        scores(0, sa_ref)

        def body(i, carry):
            kb = 2 * i
            pv(jnp.maximum(kb - 1, 0), 1)
            scores(kb + 1, sb_ref)
            softmax(0)
            pv(kb, 0)
            scores(kb + 2, sa_ref)
            softmax(1)
            return carry

        lax.fori_loop(0, qi, body, 0)
        pv(jnp.maximum(2 * qi - 1, 0), 1)
        scores(2 * qi + 1, sb_ref, r0=tk)
        softmax(0, diag=0)
        pv(2 * qi, 0)
        softmax(1, diag=1, r0=tk)
        pv(2 * qi + 1, 1, r0=tk)

    def finalize():
        lam = _lam(lq1_ref, lk1_ref, lq2_ref, lk2_ref)
        o = acc_ref[0] * _lane_tile(1.0 / l_ref[0], DA_DV) - acc_ref[1] * _lane_tile(lam / l_ref[1], DA_DV)
        o_ref[pl.ds(q_lo, tq), :] = (_rms(o) * (gd_ref[...] * (1.0 - LAM_INIT))).astype(bf16)

    return fast_path, safe_path, finalize


def _attn_prompt_kernel(q_ref, k_ref, v_ref, lq1_ref, lk1_ref, lq2_ref, lk2_ref, gd_ref, gq_ref, gk_ref,
                        dmask_ref, o_ref, *scratch, tk):
    gq_max = jnp.max(jnp.abs(gq_ref[...]), axis=1, keepdims=True)
    gk_max = jnp.max(jnp.abs(gk_ref[...]), axis=1, keepdims=True)
    fast = (gq_max * gk_max)[0, 0] * SCORE_BOUND_PER_GAIN <= FAST_SCORE_MAX
    blocks = [_attn_q_block(half, q_ref, k_ref, v_ref, lq1_ref, lk1_ref, lq2_ref, lk2_ref, gd_ref,
                            dmask_ref, o_ref, *scratch, tk=tk) for half in range(Q_BLOCKS_PER_STEP)]

    @pl.when(fast)
    def _():
        for fast_path, _, finalize in blocks:
            fast_path()
            finalize()

    @pl.when(jnp.logical_not(fast))
    def _():
        def one_block(half, carry):
            _, safe_path, finalize = _attn_q_block(half, q_ref, k_ref, v_ref, lq1_ref, lk1_ref, lq2_ref,
                                                   lk2_ref, gd_ref, dmask_ref, o_ref, *scratch, tk=tk)
            safe_path()
            finalize()
            return carry

        lax.fori_loop(0, Q_BLOCKS_PER_STEP, one_block, 0)


def _attn_prompt(dq, dkb, dvb, lam_params, g_dnorm, g_qn, g_kn, *, nb, seq, tk):
    hw = 2 * DA_DK
    tq = 2 * tk
    rows = Q_BLOCKS_PER_STEP * tq
    qspec = pl.BlockSpec((rows, hw), lambda b, h, i: (b * (seq // rows) + i, h))
    kvspec = pl.BlockSpec((seq, hw), lambda b, h, i: (b, h))
    vec = _const_spec((1, DA_DK))
    chunk_of = jnp.arange(tk) // CHUNK
    dmask = jnp.where(chunk_of[None, :] <= chunk_of[:, None], 0.0, -1e30).astype(f32)
    return pl.pallas_call(
        functools.partial(_attn_prompt_kernel, tk=tk),
        grid=(nb, DA_HEADS, seq // rows),
        in_specs=[qspec, kvspec, kvspec, vec, vec, vec, vec, _const_spec((1, DA_DV)), vec, vec,
                  _const_spec((tk, tk))],
        out_specs=qspec,
        out_shape=jax.ShapeDtypeStruct(dq.shape, bf16),
        scratch_shapes=[pltpu.VMEM((2, tq, LANES), f32), pltpu.VMEM((2, tq, LANES), f32),
                        pltpu.VMEM((2, tq, DA_DV), f32), pltpu.VMEM((2, tq, tk), f32),
                        pltpu.VMEM((2, tq, tk), f32), pltpu.VMEM((4, tq, LANES), f32),
                        pltpu.VMEM((4, tq, tk), bf16)],
        compiler_params=_params(("parallel", "parallel", "arbitrary")),
        name="attn_prompt",
    )(dq, dkb, dvb, *lam_params, g_dnorm, g_qn, g_kn, dmask)


CACHE_BUFS = 3


def _attn_decode_kernel(q_ref, kc_hbm, vc_hbm, kn_ref, vn_ref, lq1_ref, lk1_ref, lq2_ref, lk2_ref,
                        gd_ref, o_ref, kbuf_ref, vbuf_ref, sem_ref, *, past, nb):
    rows_per_token = DA_HEADS * 2
    rows = past * rows_per_token
    b = pl.program_id(0)

    def copies(stream):
        slot = lax.rem(stream, CACHE_BUFS)
        src = pl.ds(pl.multiple_of(stream * rows, rows), rows)
        return (pltpu.make_async_copy(kc_hbm.at[src], kbuf_ref.at[slot], sem_ref.at[0, slot]),
                pltpu.make_async_copy(vc_hbm.at[src], vbuf_ref.at[slot], sem_ref.at[1, slot]))

    def start(stream):
        for c in copies(stream):
            c.start()

    @pl.when(b == 0)
    def _():
        for stream in range(min(CACHE_BUFS - 1, nb)):
            start(stream)

    @pl.when(b + CACHE_BUFS - 1 < nb)
    def _():
        start(b + CACHE_BUFS - 1)

    for c in copies(b):
        c.wait()
    kc_ref = kbuf_ref.at[lax.rem(b, CACHE_BUFS)]
    vc_ref = vbuf_ref.at[lax.rem(b, CACHE_BUFS)]

    lam = _lam(lq1_ref, lk1_ref, lq2_ref, lk2_ref)
    for h in range(DA_HEADS):
        hs = slice(h * DA_DV, (h + 1) * DA_DV)
        probs = []
        for a in range(2):
            cols = slice(h * DA_DV + a * DA_DK, h * DA_DV + (a + 1) * DA_DK)
            q = q_ref[:, cols]
            kc = kc_ref[pl.ds(h * 2 + a, past, stride=rows_per_token), :].astype(bf16)
            kn = kn_ref[:, cols]
            sc = lax.dot_general(q, kc, NT_DIMS, preferred_element_type=f32)
            sn = lax.dot_general(q, kn, NT_DIMS, preferred_element_type=f32)
            m = jnp.maximum(jnp.max(sc, axis=1, keepdims=True), jnp.max(sn, axis=1, keepdims=True))
            pc = jnp.exp2(sc - m)
            pn = jnp.exp2(sn - m)
            inv = 1.0 / (jnp.sum(pc, axis=1, keepdims=True) + jnp.sum(pn, axis=1, keepdims=True))
            probs.append((pc * inv, pn * inv))
        ac = probs[0][0] - lam * probs[1][0]
        an = probs[0][1] - lam * probs[1][1]
        vc = jnp.concatenate([vc_ref[pl.ds(half * DA_HEADS + h, past, stride=rows_per_token), :]
                              for half in range(2)], axis=1).astype(bf16)
        o = jnp.dot(ac.astype(bf16), vc, preferred_element_type=f32)
        o = o + jnp.dot(an.astype(bf16), vn_ref[:, hs], preferred_element_type=f32)
        o_ref[:, hs] = (_rms(o) * gd_ref[...] * (1.0 - LAM_INIT)).astype(bf16)


def _attn_decode(dq_v, dkb_v, dvb_v, cache_k_rows, cache_v_rows, lam_params, g_dnorm, *, nb, t, past):
    new = pl.BlockSpec((t, IN_COLS), lambda b: (0, b))
    hbm = pl.BlockSpec(memory_space=pl.ANY)
    vec = _const_spec((1, DA_DK))
    ring = pltpu.VMEM((CACHE_BUFS, past * DA_HEADS * 2, DA_DK), f32)
    return pl.pallas_call(
        functools.partial(_attn_decode_kernel, past=past, nb=nb),
        grid=(nb,),
        in_specs=[new, hbm, hbm, new, new, vec, vec, vec, vec, _const_spec((1, DA_DV))],
        out_specs=new,
        out_shape=jax.ShapeDtypeStruct(dq_v.shape, bf16),
        scratch_shapes=[ring, ring, pltpu.SemaphoreType.DMA((2, CACHE_BUFS))],
        compiler_params=_params(("arbitrary",)),
        name="attn_decode",
    )(dq_v, cache_k_rows, cache_v_rows, dkb_v, dvb_v, *lam_params, g_dnorm)


def _merge_kernel(x_ref, oa_ref, ob_ref, ga_ref, gb_ref, wr_ref, wd_ref, wo_ref, y_ref):
    a = jnp.dot(oa_ref[...], wr_ref[...], preferred_element_type=f32)
    b = jnp.dot(ob_ref[...], wd_ref[...], preferred_element_type=f32)
    m = ga_ref[...].astype(f32) * a + gb_ref[...].astype(f32) * b
    y_ref[...] = x_ref[...] + jnp.dot(m.astype(bf16), wo_ref[...], preferred_element_type=f32)


def _merge(x2d, oa, ob, ga, gb, w_ret_br, w_da_br, w_out, *, tm):
    n = x2d.shape[0]
    row = pl.BlockSpec((tm, D_MODEL), lambda i: (i, 0))
    w = _const_spec((D_MODEL, D_MODEL))
    return pl.pallas_call(
        _merge_kernel,
        grid=(n // tm,),
        in_specs=[row] * 5 + [w] * 3,
        out_specs=row,
        out_shape=jax.ShapeDtypeStruct(x2d.shape, f32),
        compiler_params=_params(("parallel",)),
        name="merge",
    )(x2d, oa, ob, ga, gb, w_ret_br, w_da_br, w_out)


FF_CHUNK = 256


U_CHUNKS = 2 * D_FF // LANES


def _ffn_kernel(x_ref, ple_ref, cst_ref, gffn_ref, wup_ref, cw_ref, cb_ref, wdn_ref, gple_ref,
                wpg_ref, wple_ref, y_ref, cnew_ref, uext_ref, act_ref, *, tm, shift, halo, chunk_rows):
    first = pl.program_id(1) == 0
    x1 = x_ref[...]
    xn = (_rms(x1) * gffn_ref[...]).astype(bf16)
    state_rows = cst_ref.shape[1]
    lane_chunks = FF_CHUNK // LANES

    def chunk_slab(t0, c, rows):
        return pl.ds((halo + t0) * U_CHUNKS + c, rows, stride=U_CHUNKS)

    @pl.when(first)
    def _():
        if chunk_rows:
            for c in range(U_CHUNKS):
                uext_ref[chunk_slab(-halo, c, halo), :] = cst_ref[0, state_rows - halo:, c * LANES:(c + 1) * LANES]
        else:
            uext_ref[0:halo, :] = cst_ref[0]

    def up(col):
        ucol = jnp.dot(xn, wup_ref[:, col:col + FF_CHUNK], preferred_element_type=f32)
        if chunk_rows:
            for k in range(lane_chunks):
                uext_ref[chunk_slab(0, col // LANES + k, tm), :] = ucol[:, k * LANES:(k + 1) * LANES]
            cnew_ref[0, 0:state_rows - halo, col:col + FF_CHUNK] = jnp.zeros((state_rows - halo, FF_CHUNK), f32)
            cnew_ref[0, state_rows - halo:, col:col + FF_CHUNK] = ucol[tm - halo:, :]
        else:
            uext_ref[halo:halo + tm, col:col + FF_CHUNK] = ucol

    def tap(back, col):
        if chunk_rows:
            return jnp.concatenate([uext_ref[chunk_slab(-back * shift, col // LANES + k, tm), :]
                                    for k in range(lane_chunks)], axis=1)
        return uext_ref[halo - back * shift:halo - back * shift + tm, col:col + FF_CHUNK]

    def conv(col):
        sl = slice(col, col + FF_CHUNK)
        acc = cw_ref[0:1, sl] * tap(2, col)
        acc = acc + cw_ref[1:2, sl] * tap(1, col)
        acc = acc + cw_ref[2:3, sl] * tap(0, col)
        return acc + cb_ref[:, sl]

    for col in range(0, D_FF, FF_CHUNK):
        up(col)
        up(D_FF + col)
        act_ref[:, col:col + FF_CHUNK] = (_silu(conv(col)) * conv(D_FF + col)).astype(bf16)

    x2 = x1 + jnp.dot(act_ref[...], wdn_ref[...], preferred_element_type=f32)
    if chunk_rows:
        keep = halo * U_CHUNKS
        uext_ref[0:keep, :] = uext_ref[tm * U_CHUNKS:tm * U_CHUNKS + keep, :]
    else:
        tail = uext_ref[tm:tm + halo, :]
        cnew_ref[0] = tail
        uext_ref[0:halo, :] = tail

    gate = jnp.dot((_rms(x2) * gple_ref[...]).astype(bf16), wpg_ref[...], preferred_element_type=f32)
    emb = jnp.dot(ple_ref[...].astype(bf16), wple_ref[...], preferred_element_type=f32)
    y_ref[...] = x2 + _sigmoid(gate) * emb


def _ffn(x1, ple2d, cstate, g_ffn, w_up, conv_w, conv_b, w_down, g_ple, w_ple_gate, w_ple, *,
         tm, shift):
    n = x1.shape[0]
    nseg, crows, _ = cstate.shape
    chunk_rows = shift % SUBLANES != 0
    halo = (CONV_W - 1) * shift
    tiles = n // (nseg * tm)
    ple_dim = ple2d.shape[1]
    resident = _resident
    row = lambda width: pl.BlockSpec((tm, width), lambda s, t: (s * tiles + t, 0))
    seg = pl.BlockSpec((1, crows, 2 * D_FF), lambda s, t: (s, 0, 0))
    uext = ((halo + tm) * U_CHUNKS, LANES) if chunk_rows else (halo + tm, 2 * D_FF)
    return pl.pallas_call(
        functools.partial(_ffn_kernel, tm=tm, shift=shift, halo=halo, chunk_rows=chunk_rows),
        grid=(nseg, tiles),
        in_specs=[row(D_MODEL), row(ple_dim), seg, resident((1, D_MODEL)),
                  resident((D_MODEL, 2 * D_FF)), resident((CONV_W, 2 * D_FF)),
                  resident((1, 2 * D_FF)), resident((D_FF, D_MODEL)), resident((1, D_MODEL)),
                  resident((D_MODEL, D_MODEL)), resident((ple_dim, D_MODEL))],
        out_specs=[row(D_MODEL), seg],
        out_shape=[jax.ShapeDtypeStruct(x1.shape, f32), jax.ShapeDtypeStruct(cstate.shape, f32)],
        scratch_shapes=[pltpu.VMEM(uext, f32), pltpu.VMEM((tm, D_FF), bf16)],
        compiler_params=_params(("parallel", "arbitrary")),
        name="ffn",
    )(x1, ple2d, cstate, g_ffn, w_up, conv_w, conv_b, w_down, g_ple, w_ple_gate, w_ple)


def _rope_freqs():
    return ROPE_THETA ** (-jnp.arange(0, RET_DK, 2, dtype=f32) / RET_DK)


def _rope_finish(cos_half, sin_half):
    sign = jnp.where(jnp.arange(RET_DK) < RET_DK // 2, -1.0, 1.0).astype(f32)
    return (jnp.concatenate([cos_half, cos_half], axis=-1),
            jnp.concatenate([sin_half, sin_half], axis=-1) * sign)


def _rope_tables(pos):
    ang = pos.astype(f32)[:, None] * _rope_freqs()[None, :]
    return _rope_finish(jnp.cos(ang), jnp.sin(ang))


ROPE_SPLIT = 64


def _rope_tables_range(seq):
    inv = _rope_freqs()[None, :]
    hi = (ROPE_SPLIT * jnp.arange(seq // ROPE_SPLIT, dtype=f32))[:, None] * inv
    lo = jnp.arange(ROPE_SPLIT, dtype=f32)[:, None] * inv
    ch, sh = jnp.cos(hi)[:, None, :], jnp.sin(hi)[:, None, :]
    cl, sl = jnp.cos(lo)[None, :, :], jnp.sin(lo)[None, :, :]
    cos =(ch * cl - sh * sl).reshape(seq, RET_DK // 2)
    sin = (sh * cl + ch * sl).reshape(seq, RET_DK // 2)
    return _rope_finish(cos, sin)


def _layer_prompt(x, ple, w, *, tm_in=512, tm_merge=1024, tm_ffn=512, ret_chunk=256, attn_tk=512):
    nb, seq, _ = x.shape
    n = nb * seq
    tm_in, tm_merge, tm_ffn = min(tm_in, seq), min(tm_merge, seq), min(tm_ffn, seq)
    ret_chunk, attn_tk = min(ret_chunk, seq), min(attn_tk, seq // (2 * Q_BLOCKS_PER_STEP))
    x2d = x.reshape(n, D_MODEL)
    cos, sin = _rope_tables_range(seq)
    rqk, rv, rg, dq, dk, dkb, dv, dvb, ga, gb = _in_proj(
        x2d, w["g_mix"], w["w_in"], cos, sin, w["g_qn"], w["g_kn"], tm=tm_in)

    n_chunks = seq // ret_chunk
    s0 = jnp.zeros((nb, RET_HEADS, RET_DK, RET_DV), f32)
    per_stream = lambda a: a.reshape(nb, seq, IN_COLS)
    oa, s_new = _retention(per_stream(rqk), per_stream(rv), per_stream(rg), s0, chunk=ret_chunk,
                           n_chunks=n_chunks, stream_major=True)
    oa = oa.reshape(n, IN_COLS)
    ob = _attn_prompt(dq, dkb, dvb, w["lam"], w["g_dnorm"], w["g_qn"], w["g_kn"],
                      nb=nb, seq=seq, tk=attn_tk)
    x1 = _merge(x2d, oa, ob, ga, gb, w["w_ret_br"], w["w_da_br"], w["w_out"], tm=tm_merge)

    cstate = jnp.zeros((nb, SUBLANES, 2 * D_FF), f32)
    y, ctail = _ffn(x1, ple.reshape(n, -1), cstate, w["g_ffn"], w["w_up"], w["conv_w"], w["conv_b"],
                    w["w_down"], w["g_ple"], w["w_ple_gate"], w["w_ple"], tm=tm_ffn, shift=1)
    return (y.reshape(nb, seq, D_MODEL),
            dk.reshape(nb, seq, DA_HEADS, 2, DA_DK),
            jnp.swapaxes(dv.reshape(nb, seq, 2, DA_HEADS, LANES), 2, 3).reshape(nb, seq, DA_HEADS, DA_DV),
            s_new, ctail[:, SUBLANES - (CONV_W - 1):])


def _layer_decode(x, ple, cache_k, cache_v, state_ret, state_conv, w, *, past_len):
    nb, t, _ = x.shape
    n = nb * t
    past = cache_k.shape[1]
    tm = n

    def to_tm(a):
        return jnp.swapaxes(a, 0, 1).reshape(n, a.shape[-1])

    def from_tm(a):
        return jnp.swapaxes(a.reshape(t, nb, a.shape[-1]), 0, 1)

    cos, sin = _rope_tables(past_len + jnp.arange(n) // nb)
    rqk, rv, rg, dq, dk, dkb, dv, dvb, ga, gb = _in_proj(
        to_tm(x), w["g_mix"], w["w_in"], cos, sin, w["g_qn"], w["g_kn"], tm=tm)

    view = lambda a: a.reshape(t, nb * IN_COLS)
    oa, s_new = _retention(view(rqk), view(rv), view(rg), state_ret, chunk=t, n_chunks=1,
                           stream_major=False)
    ob = _attn_decode(view(dq), view(dkb), view(dvb),
                      cache_k.reshape(nb * past * DA_HEADS * 2, DA_DK),
                      jnp.swapaxes(cache_v.reshape(nb, past, DA_HEADS, 2, LANES), 2, 3).reshape(
                          nb * past * DA_HEADS * 2, LANES),
                      w["lam"], w["g_dnorm"], nb=nb, t=t, past=past)
    x1 = _merge(to_tm(x), oa.reshape(n, IN_COLS), ob.reshape(n, IN_COLS), ga, gb,
                w["w_ret_br"], w["w_da_br"], w["w_out"], tm=tm)

    halo = (CONV_W - 1) * nb
    cstate = jnp.swapaxes(state_conv, 0, 1).reshape(1, halo, 2 * D_FF)
    y, ctail = _ffn(x1, to_tm(ple), cstate, w["g_ffn"], w["w_up"], w["conv_w"], w["conv_b"],
                    w["w_down"], w["g_ple"], w["w_ple_gate"], w["w_ple"], tm=tm, shift=nb)
    return (from_tm(y),
            jnp.swapaxes(dk.reshape(t, nb, DA_HEADS, 2, DA_DK), 0, 1),
            jnp.transpose(dv.reshape(t, nb, 2, DA_HEADS, LANES), (1, 0, 3, 2, 4)).reshape(
                nb, t, DA_HEADS, DA_DV),
            s_new, jnp.swapaxes(ctail.reshape(CONV_W - 1, nb, 2 * D_FF), 0, 1))


def _layer_weights(i, g_mix, w_in, g_qn, g_kn, lam_q1, lam_k1, lam_q2, lam_k2, g_dnorm, w_ret_br,
                   w_da_br, w_out, g_ffn, w_up, conv_w, conv_b, w_down, g_ple, w_ple_gate, w_ple):
    vec = lambda a: a[i].reshape(1, -1)
    mats = {k: v[i].astype(bf16) for k, v in dict(
        w_in=w_in, w_ret_br=w_ret_br, w_da_br=w_da_br, w_out=w_out, w_up=w_up, w_down=w_down,
        w_ple_gate=w_ple_gate, w_ple=w_ple).items()}
    return dict(
        g_mix=vec(g_mix), g_qn=vec(g_qn), g_kn=vec(g_kn),
        lam=(vec(lam_q1), vec(lam_k1), vec(lam_q2), vec(lam_k2)), g_dnorm=vec(g_dnorm),
        g_ffn=vec(g_ffn), conv_w=conv_w[i], conv_b=vec(conv_b), g_ple=vec(g_ple), **mats)


def kernel(x_prompt, x_sample, cache_k, cache_v, state_ret, state_conv, p_prompt, p_sample, g_mix, w_in, g_qn, g_kn, lam_q1, lam_k1, lam_q2, lam_k2, g_dnorm, w_ret_br, w_da_br, w_out, g_ffn, w_up, conv_w, conv_b, w_down, g_ple, w_ple_gate, w_ple):
    depth = w_in.shape[0]
    yp, ys = x_prompt, x_sample
    outs = [[] for _ in range(8)]
    for i in range(depth):
        w = _layer_weights(i, g_mix, w_in, g_qn, g_kn, lam_q1, lam_k1, lam_q2, lam_k2, g_dnorm,
                           w_ret_br, w_da_br, w_out, g_ffn, w_up, conv_w, conv_b, w_down, g_ple,
                           w_ple_gate, w_ple)
        yp, k1, v1, s1, c1 = _layer_prompt(yp, p_prompt[i], w)
        ys, k2, v2, s2, c2 = _layer_decode(ys, p_sample[i], cache_k[i], cache_v[i], state_ret[i],
                                           state_conv[i], w, past_len=PAST_LEN)
        for lst, val in zip(outs, (k1, k2, v1, v2, s1, s2, c1, c2)):
            lst.append(val)
    return (yp, ys) + tuple(jnp.stack(o) for o in outs)
```
